```python
import jax, jax.numpy as jnp
from jax import lax
import numpy as np

D_MODEL = 4096
BATCH = 1
SEQ = 16384
DEPTH = 1
DEC_BATCH = 16
DEC_SEQ = 32
PAST_LEN = 2048

CHUNK = 64
D_MIX = D_MODEL
D_POOL = D_MIX // 4
POOL_WINDOWS = (2, 4, 8, 16)
POOL_GROUP = D_POOL // len(POOL_WINDOWS)
POOL_STATE = max(POOL_WINDOWS) - 1
V_HEAD = 128
N_HEADS = (D_MIX - D_POOL) // V_HEAD
QK_NOPE = 128
QK_ROPE = 64
QK_HEAD = QK_NOPE + QK_ROPE
Q_LORA = D_MODEL // 4
KV_LORA = 512
ROPE_THETA = 10000.0
N_EXPERTS = 32
TOP_K = 4
D_FF = D_MODEL
SWIGLU_LIMIT = 7.0
SWIGLU_ALPHA = 1.702
EPS = 1e-6
Q_BLOCK = 128
MOE_BLOCK = 256
D_IN = D_POOL + Q_LORA + KV_LORA + QK_ROPE
SPLITS = [D_POOL, D_POOL + Q_LORA, D_POOL + Q_LORA + KV_LORA]

kernel_name = 'hybrid_pool_mla_moe_stream_step'


def rmsnorm(x, g):
    xf = x.astype(jnp.float32)
    xf = xf * lax.rsqrt(jnp.mean(xf * xf, axis=-1, keepdims=True) + EPS)
    return (xf * g.astype(jnp.float32)).astype(x.dtype)


def rope_angles(pos):
    inv = 1.0 / (ROPE_THETA ** (jnp.arange(0, QK_ROPE, 2, dtype=jnp.float32) / QK_ROPE))
    ang = pos.astype(jnp.float32)[:, None] * inv[None, :]
    return jnp.cos(ang), jnp.sin(ang)


def apply_rope(x, cos, sin):
    x1, x2 = jnp.split(x.astype(jnp.float32), 2, axis=-1)
    return jnp.concatenate([x1 * cos - x2 * sin, x1 * sin + x2 * cos], axis=-1).astype(x.dtype)


def expand_kv(c_kv, k_pe, w_kv_b, k_norm):
    B, L, _ = c_kv.shape
    kv = (c_kv @ w_kv_b).reshape(B, L, N_HEADS, QK_NOPE + V_HEAD)
    k_nope, v = kv[..., :QK_NOPE], kv[..., QK_NOPE:]
    k = jnp.concatenate([k_nope, jnp.broadcast_to(k_pe[:, :, None, :], (B, L, N_HEADS, QK_ROPE))], axis=-1)
    return rmsnorm(k, k_norm), v


def attend(q, k, v, mask):
    s = jnp.einsum('bqhd,bkhd->bhqk', q, k, preferred_element_type=jnp.float32) * (QK_HEAD ** -0.5)
    if mask is not None:
        s = jnp.where(mask[None, None], s, -jnp.inf)
    p = jax.nn.softmax(s, axis=-1)
    return jnp.einsum('bhqk,bkhd->bqhd', p.astype(v.dtype), v)


def chunk_causal_attention(q, k, v):
    B, S, H, _ = q.shape
    nb = S // Q_BLOCK
    qb = q.reshape(B, nb, Q_BLOCK, H, QK_HEAD).transpose(1, 0, 2, 3, 4)
    key_chunk = jnp.arange(S) // CHUNK

    def one_block(args):
        i, qi = args
        q_chunk = (i * Q_BLOCK + jnp.arange(Q_BLOCK)) // CHUNK
        return attend(qi, k, v, key_chunk[None, :] <= q_chunk[:, None])

    out = lax.map(one_block, (jnp.arange(nb), qb))
    return out.transpose(1, 0, 2, 3, 4).reshape(B, S, H, V_HEAD)


def multiscale_pool(u, prev, pos, w_pool, pool_scale):
    B, T, _ = u.shape
    ext = jnp.concatenate([prev, u], axis=1).astype(jnp.float32)
    cs = jnp.concatenate([jnp.zeros((B, 1, D_POOL), jnp.float32), jnp.cumsum(ext, axis=1)], axis=1)
    outs = []
    for g, w in enumerate(POOL_WINDOWS):
        sl = slice(g * POOL_GROUP, (g + 1) * POOL_GROUP)
        csg = cs[..., sl]
        win = csg[:, POOL_STATE + 1:] - csg[:, POOL_STATE + 1 - w:POOL_STATE + 1 - w + T]
        cnt = jnp.minimum(pos + 1, w).astype(jnp.float32)[None, :, None]
        pooled = win / cnt - u[..., sl].astype(jnp.float32)
        outs.append(jnp.einsum('btc,cd->btd', pooled.astype(u.dtype), w_pool[g]))
    return jnp.concatenate(outs, axis=-1) * pool_scale


def moe(x, ffn_norm, w_router, b_router, w_gate, b_gate, w_up, b_up, w_down, b_down):
    B, T, D = x.shape
    N = B * T
    NK = N * TOP_K
    h = rmsnorm(x, ffn_norm).reshape(N, D)
    logits = (h @ w_router + b_router).astype(jnp.float32)
    top_vals, top_idx = lax.top_k(logits, TOP_K)
    gates = jax.nn.softmax(top_vals, axis=-1)
    e_flat = top_idx.reshape(NK)
    order = jnp.argsort(e_flat)
    e_sorted = e_flat[order]
    counts = jnp.bincount(e_flat, length=N_EXPERTS)
    padded = (counts + MOE_BLOCK - 1) // MOE_BLOCK * MOE_BLOCK
    start = jnp.cumsum(counts) - counts
    pend = jnp.cumsum(padded)
    pstart = pend - padded
    slot_sorted = pstart[e_sorted] + jnp.arange(NK, dtype=jnp.int32) - start[e_sorted]
    slot = jnp.zeros((NK,), jnp.int32).at[order].set(slot_sorted.astype(jnp.int32))
    n_blocks = -(-NK // MOE_BLOCK) + N_EXPERTS
    S = n_blocks * MOE_BLOCK
    slot_token = jnp.full((S,), N, jnp.int32).at[slot].set(jnp.arange(NK, dtype=jnp.int32) // TOP_K)
    block_expert = jnp.minimum(jnp.searchsorted(pend, jnp.arange(n_blocks) * MOE_BLOCK, side='right'), N_EXPERTS - 1)
    h_pad = jnp.concatenate([h, jnp.zeros((1, D), h.dtype)], axis=0)
    xb = h_pad[slot_token].reshape(n_blocks, MOE_BLOCK, D)

    def expert_block(args):
        e, xe = args
        g = jnp.minimum(xe @ w_gate[e] + b_gate[e], SWIGLU_LIMIT)
        up = jnp.clip(xe @ w_up[e] + b_up[e], -SWIGLU_LIMIT, SWIGLU_LIMIT)
        act = g * jax.nn.sigmoid(SWIGLU_ALPHA * g) * (up + 1.0)
        return act @ w_down[e] + b_down[e]

    yb = lax.map(expert_block, (block_expert, xb)).reshape(S, D)
    y = jnp.einsum('nkd,nk->nd', yb[slot].reshape(N, TOP_K, D), gates.astype(yb.dtype))
    return y.reshape(B, T, D)


def trunk_layer(x, pos, pool_prev, ckv_past, kpe_past, attn_norm, w_in, q_a_norm, w_q_b, kv_a_norm, w_kv_b,
                q_norm, k_norm, w_pool, pool_scale, w_out, ffn_norm, w_router, b_router,
                w_gate, b_gate, w_up, b_up, w_down, b_down):
    B, T, _ = x.shape
    h = rmsnorm(x, attn_norm)
    u, c_q, c_kv, k_pe = jnp.split(h @ w_in, SPLITS, axis=-1)
    cos, sin = rope_angles(pos)
    q = (rmsnorm(c_q, q_a_norm) @ w_q_b).reshape(B, T, N_HEADS, QK_HEAD)
    q = jnp.concatenate([q[..., :QK_NOPE], apply_rope(q[..., QK_NOPE:], cos[:, None, :], sin[:, None, :])], axis=-1)
    q = rmsnorm(q, q_norm)
    c_kv = rmsnorm(c_kv, kv_a_norm)
    k_pe = apply_rope(k_pe, cos, sin)
    if ckv_past is None:
        k, v = expand_kv(c_kv, k_pe, w_kv_b, k_norm)
        attn = chunk_causal_attention(q, k, v)
    else:
        k, v = expand_kv(jnp.concatenate([ckv_past, c_kv], axis=1), jnp.concatenate([kpe_past, k_pe], axis=1), w_kv_b, k_norm)
        attn = attend(q, k, v, None)
    pool = multiscale_pool(u, pool_prev, pos, w_pool, pool_scale)
    x = x + jnp.concatenate([pool, attn.reshape(B, T, N_HEADS * V_HEAD)], axis=-1) @ w_out
    x = x + moe(x, ffn_norm, w_router, b_router, w_gate, b_gate, w_up, b_up, w_down, b_down)
    new_pool = jnp.concatenate([pool_prev, u], axis=1)[:, -POOL_STATE:]
    return x, c_kv, k_pe, new_pool


def setup_inputs(seed: int = 0) -> dict:
    key = jax.random.key(seed)
    ks = jax.random.split(key, 32)
    f32 = jnp.float32
    nrm = lambda k, shape, s: jax.random.normal(k, shape, f32) * s
    gain = lambda k, n: 1.0 + 0.05 * jax.random.normal(k, (n,), f32)
    return {
        'x_prompt': nrm(ks[0], (BATCH, SEQ, D_MODEL), 1.0),
        'x_sample': nrm(ks[1], (DEC_BATCH, DEC_SEQ, D_MODEL), 1.0),
        'cache_ckv': nrm(ks[2], (DEC_BATCH, PAST_LEN, KV_LORA), 1.0),
        'cache_kpe': nrm(ks[3], (DEC_BATCH, PAST_LEN, QK_ROPE), 1.0),
        'state_pool': nrm(ks[4], (DEC_BATCH, POOL_STATE, D_POOL), 1.0),
        'attn_norm': gain(ks[5], D_MODEL),
        'w_in': nrm(ks[6], (D_MODEL, D_IN), D_MODEL ** -0.5),
        'q_a_norm': gain(ks[7], Q_LORA),
        'w_q_b': nrm(ks[8], (Q_LORA, N_HEADS * QK_HEAD), Q_LORA ** -0.5),
        'kv_a_norm': gain(ks[9], KV_LORA),
        'w_kv_b': nrm(ks[10], (KV_LORA, N_HEADS * (QK_NOPE + V_HEAD)), KV_LORA ** -0.5),
        'q_norm': gain(ks[11], QK_HEAD),
        'k_norm': gain(ks[12], QK_HEAD),
        'w_pool': nrm(ks[13], (len(POOL_WINDOWS), POOL_GROUP, POOL_GROUP), POOL_GROUP ** -0.5),
        'pool_scale': gain(ks[14], D_POOL),
        'w_out': nrm(ks[15], (D_MIX, D_MODEL), D_MIX ** -0.5),
        'ffn_norm': gain(ks[16], D_MODEL),
        'w_router': nrm(ks[17], (D_MODEL, N_EXPERTS), D_MODEL ** -0.5),
        'b_router': nrm(ks[18], (N_EXPERTS,), 0.01),
        'w_gate': nrm(ks[19], (N_EXPERTS, D_MODEL, D_FF), D_MODEL ** -0.5),
        'b_gate': nrm(ks[20], (N_EXPERTS, D_FF), 0.02),
        'w_up': nrm(ks[21], (N_EXPERTS, D_MODEL, D_FF), D_MODEL ** -0.5),
        'b_up': nrm(ks[22], (N_EXPERTS, D_FF), 0.02),
        'w_down': nrm(ks[23], (N_EXPERTS, D_FF, D_MODEL), D_FF ** -0.5),
        'b_down': nrm(ks[24], (N_EXPERTS, D_MODEL), 0.02),
    }


def reference(x_prompt, x_sample, cache_ckv, cache_kpe, state_pool, attn_norm, w_in, q_a_norm, w_q_b,
              kv_a_norm, w_kv_b, q_norm, k_norm, w_pool, pool_scale, w_out, ffn_norm, w_router, b_router,
              w_gate, b_gate, w_up, b_up, w_down, b_down):
    weights = (attn_norm, w_in, q_a_norm, w_q_b, kv_a_norm, w_kv_b, q_norm, k_norm, w_pool, pool_scale,
               w_out, ffn_norm, w_router, b_router, w_gate, b_gate, w_up, b_up, w_down, b_down)
    B_p, T_p, _ = x_prompt.shape
    pool_zero = jnp.zeros((B_p, POOL_STATE, D_POOL), x_prompt.dtype)
    y_prompt, ckv_p, kpe_p, pool_p = trunk_layer(x_prompt, jnp.arange(T_p), pool_zero, None, None, *weights)
    past = cache_ckv.shape[1]
    T_s = x_sample.shape[1]
    y_sample, ckv_s, kpe_s, pool_s = trunk_layer(x_sample, past + jnp.arange(T_s), state_pool, cache_ckv, cache_kpe, *weights)
    return (y_prompt, y_sample, ckv_p, kpe_p, pool_p, ckv_s, kpe_s, pool_s)
```

```python
import functools

import jax
import jax.numpy as jnp
from jax import lax
from jax.experimental import pallas as pl
from jax.experimental.pallas import tpu as pltpu

CHUNK = 64
POOL_WINDOWS = (2, 4, 8, 16)
POOL_STATE = max(POOL_WINDOWS) - 1
POOL_HALO = POOL_STATE + 1
QK_NOPE = 128
QK_ROPE = 64
QK_HEAD = QK_NOPE + QK_ROPE
V_HEAD = 128
ROPE_THETA = 10000.0
TOP_K = 4
SWIGLU_LIMIT = 7.0
SWIGLU_ALPHA = 1.702
EPS = 1e-6

F32 = jnp.float32
BF16 = jnp.bfloat16
MIB = 1024 * 1024
NT_DIMS = (((1,), (1,)), ((), ()))


def _params(semantics, vmem_mib=None):
    kw = dict(dimension_semantics=semantics)
    if vmem_mib is not None:
        kw["vmem_limit_bytes"] = vmem_mib * MIB
    return pltpu.CompilerParams(**kw)


def _tile(n, pref, mult=16):
    t = min(pref, n)
    t -= t % mult
    while t > mult and n % t:
        t -= mult
    assert t >= mult and n % t == 0, (n, pref, mult)
    return t


def _inproj_kernel(x_ref, g_ref, w_ref, qg_ref, kvg_ref, cos_ref, sin_ref,
                   u_ref, cqn_ref, ckv_ref, kpe_ref, h_sc, seg_sc, *, nu, nq, nkv, tn):
    j = pl.program_id(1)

    @pl.when(j == 0)
    def _():
        x = x_ref[...]
        ms = jnp.mean(x * x, axis=-1, keepdims=True)
        h_sc[...] = (x * lax.rsqrt(ms + EPS) * g_ref[...]).astype(BF16)

    y = jnp.dot(h_sc[...], w_ref[...], preferred_element_type=F32)

    @pl.when(j < nu)
    def _():
        u_ref[...] = y

    def finish(n, gain_ref, out_ref):
        parts = [seg_sc[t] for t in range(n - 1)] + [y]
        ss = jnp.sum(parts[0] * parts[0], axis=-1, keepdims=True)
        for p in parts[1:]:
            ss = ss + jnp.sum(p * p, axis=-1, keepdims=True)
        r = lax.rsqrt(ss / (n * tn) + EPS)
        for t, p in enumerate(parts):
            out_ref[:, t * tn:(t + 1) * tn] = (p * r * gain_ref[:, t * tn:(t + 1) * tn]).astype(out_ref.dtype)

    def segment(first, n, gain_ref, out_ref):
        for t in range(n - 1):
            @pl.when(j == first + t)
            def _(t=t):
                seg_sc[t] = y

        @pl.when(j == first + n - 1)
        def _():
            finish(n, gain_ref, out_ref)

    segment(nu, nq, qg_ref, cqn_ref)
    segment(nu + nq, nkv, kvg_ref, ckv_ref)

    @pl.when(j == nu + nq + nkv)
    def _():
        kpe_ref[...] = y[:, :QK_ROPE] * cos_ref[...] + y[:, QK_ROPE:2 * QK_ROPE] * sin_ref[...]


def _inproj(x, attn_norm, w_all, q_a_norm, kv_a_norm, cos2, sin2, *, d_pool, q_lora, kv_lora, tn, tm):
    n, d = x.shape
    nu, nq, nkv = d_pool // tn, q_lora // tn, kv_lora // tn
    nj = nu + nq + nkv + 1
    assert w_all.shape == (d, nj * tn)
    kern = functools.partial(_inproj_kernel, nu=nu, nq=nq, nkv=nkv, tn=tn)
    return pl.pallas_call(
        kern,
        grid=(n // tm, nj),
        in_specs=[
            pl.BlockSpec((tm, d), lambda i, j: (i, 0)),
            pl.BlockSpec((1, d), lambda i, j: (0, 0)),
            pl.BlockSpec((d, tn), lambda i, j: (0, j)),
            pl.BlockSpec((1, q_lora), lambda i, j: (0, 0)),
            pl.BlockSpec((1, kv_lora), lambda i, j: (0, 0)),
            pl.BlockSpec((tm, QK_ROPE), lambda i, j: (i, 0)),
            pl.BlockSpec((tm, QK_ROPE), lambda i, j: (i, 0)),
        ],
        out_specs=[
            pl.BlockSpec((tm, tn), lambda i, j: (i, jnp.minimum(j, nu - 1))),
            pl.BlockSpec((tm, q_lora), lambda i, j: (i, 0)),
            pl.BlockSpec((tm, kv_lora), lambda i, j: (i, 0)),
            pl.BlockSpec((tm, QK_ROPE), lambda i, j: (i, 0)),
        ],
        out_shape=[
            jax.ShapeDtypeStruct((n, d_pool), F32),
            jax.ShapeDtypeStruct((n, q_lora), BF16),
            jax.ShapeDtypeStruct((n, kv_lora), F32),
            jax.ShapeDtypeStruct((n, QK_ROPE), F32),
        ],
        scratch_shapes=[
            pltpu.VMEM((tm, d), BF16),
            pltpu.VMEM((max(nq, nkv, 2) - 1, tm, tn), F32),
        ],
        compiler_params=_params(("parallel", "arbitrary"), 48),
        name="inproj",
    )(x, attn_norm, w_all, q_a_norm, kv_a_norm, cos2, sin2)


def _qproj_kernel(c_ref, w_ref, g_ref, cos_ref, sin_ref, q_ref, *, hb, scale):
    c = c_ref[...]
    cos = cos_ref[...]
    sin = sin_ref[...]
    g = g_ref[...]
    for h in range(hb):
        y = jnp.dot(c, w_ref[h], preferred_element_type=F32)
        nope = y[:, :QK_NOPE]
        rope = y[:, QK_NOPE:QK_HEAD] * cos + y[:, QK_HEAD:QK_HEAD + QK_ROPE] * sin
        ss = jnp.sum(nope * nope, axis=-1, keepdims=True) + jnp.sum(rope * rope, axis=-1, keepdims=True)
        r = lax.rsqrt(ss / QK_HEAD + EPS) * scale
        q_ref[h, :, :QK_NOPE] = (nope * r * g[:, :QK_NOPE]).astype(BF16)
        q_ref[h, :, QK_NOPE:] = (rope * r * g[:, QK_NOPE:]).astype(BF16)


def _qproj(cqn, wq, q_norm, cos2, sin2, *, tm, hb):
    n, q_lora = cqn.shape
    nh = wq.shape[0]
    kern = functools.partial(_qproj_kernel, hb=hb, scale=QK_HEAD ** -0.5)
    return pl.pallas_call(
        kern,
        grid=(n // tm, nh // hb),
        in_specs=[
            pl.BlockSpec((tm, q_lora), lambda i, j: (i, 0)),
            pl.BlockSpec((hb, q_lora, QK_HEAD + QK_ROPE), lambda i, j: (j, 0, 0)),
            pl.BlockSpec((1, QK_HEAD), lambda i, j: (0, 0)),
            pl.BlockSpec((tm, QK_ROPE), lambda i, j: (i, 0)),
            pl.BlockSpec((tm, QK_ROPE), lambda i, j: (i, 0)),
        ],
        out_specs=pl.BlockSpec((hb, tm, QK_HEAD), lambda i, j: (j, i, 0)),
        out_shape=jax.ShapeDtypeStruct((nh, n, QK_HEAD), BF16),
        compiler_params=_params(("parallel", "arbitrary"), 48),
        name="qproj",
    )(cqn, wq, q_norm, cos2, sin2)


def _kvexp_kernel(c_ref, pe_ref, w_ref, g_ref, k_ref, v_ref, *, hb):
    c = c_ref[...].astype(BF16)
    pe = pe_ref[...]
    g = g_ref[...]
    pe_ss = jnp.sum(pe * pe, axis=-1, keepdims=True)
    for h in range(hb):
        y = jnp.dot(c, w_ref[h], preferred_element_type=F32)
        nope = y[:, :QK_NOPE]
        r = lax.rsqrt((jnp.sum(nope * nope, axis=-1, keepdims=True) + pe_ss) / QK_HEAD + EPS)
        k_ref[h, :, :QK_NOPE] = (nope * r * g[:, :QK_NOPE]).astype(BF16)
        k_ref[h, :, QK_NOPE:] = (pe * r * g[:, QK_NOPE:]).astype(BF16)
        v_ref[h] = y[:, QK_NOPE:].astype(BF16)


def _kvexp(ckv, kpe, wkv, k_norm, *, tm, hb):
    n, kv_lora = ckv.shape
    nh = wkv.shape[0]
    kern = functools.partial(_kvexp_kernel, hb=hb)
    return pl.pallas_call(
        kern,
        grid=(n // tm, nh // hb),
        in_specs=[
            pl.BlockSpec((tm, kv_lora), lambda i, j: (i, 0)),
            pl.BlockSpec((tm, QK_ROPE), lambda i, j: (i, 0)),
            pl.BlockSpec((hb, kv_lora, QK_NOPE + V_HEAD), lambda i, j: (j, 0, 0)),
            pl.BlockSpec((1, QK_HEAD), lambda i, j: (0, 0)),
        ],
        out_specs=[
            pl.BlockSpec((hb, tm, QK_HEAD), lambda i, j: (j, i, 0)),
            pl.BlockSpec((hb, tm, V_HEAD), lambda i, j: (j, i, 0)),
        ],
        out_shape=[
            jax.ShapeDtypeStruct((nh, n, QK_HEAD), BF16),
            jax.ShapeDtypeStruct((nh, n, V_HEAD), BF16),
        ],
        compiler_params=_params(("parallel", "arbitrary"), 48),
        name="kvexp",
    )(ckv, kpe, wkv, k_norm)


def _attn_prompt_kernel(q_ref, k_ref, v_ref, o_ref, m_sc, l_sc, acc_sc, *, tq):
    i = pl.program_id(1)
    q = q_ref[0]
    m_sc[...] = jnp.full(m_sc.shape, -jnp.inf, F32)
    l_sc[...] = jnp.zeros(l_sc.shape, F32)
    acc_sc[...] = jnp.zeros(acc_sc.shape, F32)

    def update(s, v):
        m_old = m_sc[...]
        m_new = jnp.maximum(m_old, jnp.max(s, axis=-1, keepdims=True))
        alpha = jnp.exp(m_old - m_new)
        p = jnp.exp(s - m_new)
        l_sc[...] = alpha * l_sc[...] + jnp.sum(p, axis=-1, keepdims=True)
        acc_sc[...] = alpha * acc_sc[...] + jnp.dot(p.astype(BF16), v, preferred_element_type=F32)
        m_sc[...] = m_new

    def scores(c):
        off = pl.multiple_of(c * tq, tq)
        k = k_ref[0, pl.ds(off, tq), :]
        v = v_ref[0, pl.ds(off, tq), :]
        return lax.dot_general(q, k, NT_DIMS, preferred_element_type=F32), v

    def body(c, carry):
        s, v = scores(c)
        update(s, v)
        return carry

    lax.fori_loop(0, i, body, 0)

    s, v = scores(i)
    row_chunk = lax.broadcasted_iota(jnp.int32, s.shape, 0) // CHUNK
    col_chunk = lax.broadcasted_iota(jnp.int32, s.shape, 1) // CHUNK
    update(jnp.where(col_chunk <= row_chunk, s, -jnp.inf), v)

    o_ref[...] = (acc_sc[...] / l_sc[...]).astype(o_ref.dtype)


def _attn_prompt(q, k, v, *, seq, tq):
    nh = q.shape[0]
    assert seq % tq == 0 and tq % CHUNK == 0
    kern = functools.partial(_attn_prompt_kernel, tq=tq)
    return pl.pallas_call(
        kern,
        grid=(nh, seq // tq),
        in_specs=[
            pl.BlockSpec((1, tq, QK_HEAD), lambda h, i: (h, i, 0)),
            pl.BlockSpec((1, seq, QK_HEAD), lambda h, i: (h, 0, 0)),
            pl.BlockSpec((1, seq, V_HEAD), lambda h, i: (h, 0, 0)),
        ],
        out_specs=pl.BlockSpec((tq, V_HEAD), lambda h, i: (i, h)),
        out_shape=jax.ShapeDtypeStruct((seq, nh * V_HEAD), BF16),
        scratch_shapes=[
            pltpu.VMEM((tq, 1), F32),
            pltpu.VMEM((tq, 1), F32),
            pltpu.VMEM((tq, V_HEAD), F32),
        ],
        compiler_params=_params(("parallel", "arbitrary"), 48),
        name="attn_prompt",
    )(q, k, v)


def _attn_sample_kernel(q_ref, kp_ref, vp_ref, kn_ref, vn_ref, o_ref):
    q = q_ref[0]
    s1 = lax.dot_general(q, kp_ref[0], NT_DIMS, preferred_element_type=F32)
    s2 = lax.dot_general(q, kn_ref[0], NT_DIMS, preferred_element_type=F32)
    m = jnp.maximum(jnp.max(s1, axis=-1, keepdims=True), jnp.max(s2, axis=-1, keepdims=True))
    p1 = jnp.exp(s1 - m)
    p2 = jnp.exp(s2 - m)
    l = jnp.sum(p1, axis=-1, keepdims=True) + jnp.sum(p2, axis=-1, keepdims=True)
    o = (jnp.dot(p1.astype(BF16), vp_ref[0], preferred_element_type=F32)
         + jnp.dot(p2.astype(BF16), vn_ref[0], preferred_element_type=F32))
    o_ref[...] = (o / l).astype(o_ref.dtype)


def _attn_sample(q, k_new, v_new, k_past, v_past, *, nb, t, past, row0):
    nh = q.shape[0]
    assert row0 % t == 0
    r0 = row0 // t
    return pl.pallas_call(
        _attn_sample_kernel,
        grid=(nh, nb),
        in_specs=[
            pl.BlockSpec((1, t, QK_HEAD), lambda h, b: (h, r0 + b, 0)),
            pl.BlockSpec((1, past, QK_HEAD), lambda h, b: (h, b, 0)),
            pl.BlockSpec((1, past, V_HEAD), lambda h, b: (h, b, 0)),
            pl.BlockSpec((1, t, QK_HEAD), lambda h, b: (h, r0 + b, 0)),
            pl.BlockSpec((1, t, V_HEAD), lambda h, b: (h, r0 + b, 0)),
        ],
        out_specs=pl.BlockSpec((t, V_HEAD), lambda h, b: (b, h)),
        out_shape=jax.ShapeDtypeStruct((nb * t, nh * V_HEAD), BF16),
        compiler_params=_params(("parallel", "arbitrary"), 48),
        name="attn_sample",
    )(q, k_past, v_past, k_new, v_new)


def _pool_kernel(u_ref, prev_ref, w_ref, sc_ref, o_ref, ext_sc, *, tm, pg, pos0):
    t = pl.program_id(1)

    @pl.when(t == 0)
    def _():
        ext_sc[0:POOL_HALO, :] = prev_ref[0]

    @pl.when(t > 0)
    def _():
        ext_sc[0:POOL_HALO, :] = ext_sc[tm:tm + POOL_HALO, :]

    u = u_ref[0]
    ext_sc[POOL_HALO:, :] = u
    pos = pos0 + t * tm + lax.broadcasted_iota(jnp.int32, (tm, 1), 0)
    for g, w in enumerate(POOL_WINDOWS):
        cols = slice(g * pg, (g + 1) * pg)
        win = ext_sc[POOL_HALO:POOL_HALO + tm, cols]
        for back in range(1, w):
            win = win + ext_sc[POOL_HALO - back:POOL_HALO - back + tm, cols]
        inv_cnt = 1.0 / jnp.minimum(pos + 1, w).astype(F32)
        pooled = win * inv_cnt - u[:, cols]
        y = jnp.dot(pooled.astype(BF16), w_ref[g], preferred_element_type=F32)
        o_ref[:, cols] = (y * sc_ref[:, cols]).astype(o_ref.dtype)


def _pool(u, prev_halo, w_pool, pool_scale, *, pos0, tm):
    b, t, c = u.shape
    pg = c // len(POOL_WINDOWS)
    nt = t // tm
    kern = functools.partial(_pool_kernel, tm=tm, pg=pg, pos0=pos0)
    return pl.pallas_call(
        kern,
        grid=(b, nt),
        in_specs=[
            pl.BlockSpec((1, tm, c), lambda bi, ti: (bi, ti, 0)),
            pl.BlockSpec((1, POOL_HALO, c), lambda bi, ti: (bi, 0, 0)),
            pl.BlockSpec((len(POOL_WINDOWS), pg, pg), lambda bi, ti: (0, 0, 0)),
            pl.BlockSpec((1, c), lambda bi, ti: (0, 0)),
        ],
        out_specs=pl.BlockSpec((tm, c), lambda bi, ti: (bi * nt + ti, 0)),
        out_shape=jax.ShapeDtypeStruct((b * t, c), BF16),
        scratch_shapes=[pltpu.VMEM((tm + POOL_HALO, c), F32)],
        compiler_params=_params(("arbitrary", "arbitrary"), 48),
        name="pool",
    )(u, prev_halo, w_pool, pool_scale)


def _outproj_kernel(p_ref, a_ref, wp_ref, wa_ref, x_ref, o_ref):
    acc = jnp.dot(p_ref[...], wp_ref[...], preferred_element_type=F32)
    acc = acc + jnp.dot(a_ref[...], wa_ref[...], preferred_element_type=F32)
    o_ref[...] = x_ref[...] + acc


def _outproj(pool, attn, w_p, w_a, x, *, tm, tn):
    n, d = x.shape
    dp, da = pool.shape[1], attn.shape[1]
    return pl.pallas_call(
        _outproj_kernel,
        grid=(d // tn, n // tm),
        in_specs=[
            pl.BlockSpec((tm, dp), lambda j, i: (i, 0)),
            pl.BlockSpec((tm, da), lambda j, i: (i, 0)),
            pl.BlockSpec((dp, tn), lambda j, i: (0, j)),
            pl.BlockSpec((da, tn), lambda j, i: (0, j)),
            pl.BlockSpec((tm, tn), lambda j, i: (i, j)),
        ],
        out_specs=pl.BlockSpec((tm, tn), lambda j, i: (i, j)),
        out_shape=jax.ShapeDtypeStruct((n, d), F32),
        compiler_params=_params(("parallel", "arbitrary"), 48),
        name="outproj",
    )(pool, attn, w_p, w_a, x)


def _router_kernel(x_ref, g_ref, wr_ref, br_ref, hp_ref, idx_ref, gate_ref, rank_ref, cnt_ref, run_sc, *, ne):
    i = pl.program_id(0)

    @pl.when(i == 0)
    def _():
        run_sc[...] = jnp.zeros(run_sc.shape, F32)

    x = x_ref[...]
    tm, d = x.shape
    ms = jnp.mean(x * x, axis=-1, keepdims=True)
    h = x * lax.rsqrt(ms + EPS) * g_ref[...]
    hb = h.astype(BF16)

    bits = lax.bitcast_convert_type(hb.astype(F32), jnp.uint32)
    hp_ref[...] = (bits[:, :d // 2] >> 16) | (bits[:, d // 2:] & jnp.uint32(0xFFFF0000))

    logits = lax.dot_general(wr_ref[...], hb, NT_DIMS, preferred_element_type=F32) + br_ref[...]
    eidx = lax.broadcasted_iota(jnp.int32, logits.shape, 0)
    work = logits
    vals, idxs = [], []
    for _ in range(TOP_K):
        m = jnp.max(work, axis=0, keepdims=True)
        sel = jnp.min(jnp.where(work == m, eidx, ne), axis=0, keepdims=True)
        vals.append(m)
        idxs.append(sel)
        work = jnp.where(eidx == sel, -jnp.inf, work)
    exps = [jnp.exp(v - vals[0]) for v in vals]
    den = exps[0]
    for e in exps[1:]:
        den = den + e

    hits = [(eidx == sel).astype(F32) for sel in idxs]
    onehot = hits[0]
    for hit in hits[1:]:
        onehot = onehot + hit
    before = (lax.broadcasted_iota(jnp.int32, (tm, tm), 0) < lax.broadcasted_iota(jnp.int32, (tm, tm), 1))
    prefix = jnp.dot(onehot.astype(BF16), before.astype(BF16), preferred_element_type=F32)
    base = prefix + run_sc[...]
    for k in range(TOP_K):
        idx_ref[k:k + 1, :] = idxs[k]
        gate_ref[k:k + 1, :] = exps[k] / den
        rank_ref[k:k + 1, :] = jnp.sum(hits[k] * base, axis=0, keepdims=True).astype(jnp.int32)
    run_sc[...] = run_sc[...] + jnp.sum(onehot, axis=1, keepdims=True)
    cnt_ref[...] = run_sc[...].astype(jnp.int32)


def _router(x1, ffn_norm, wr_t, b_router, *, tm):
    n, d = x1.shape
    ne = wr_t.shape[0]
    kern = functools.partial(_router_kernel, ne=ne)
    return pl.pallas_call(
        kern,
        grid=(n // tm,),
        in_specs=[
            pl.BlockSpec((tm, d), lambda i: (i, 0)),
            pl.BlockSpec((1, d), lambda i: (0, 0)),
            pl.BlockSpec((ne, d), lambda i: (0, 0)),
            pl.BlockSpec((ne, 1), lambda i: (0, 0)),
        ],
        out_specs=[
            pl.BlockSpec((tm, d // 2), lambda i: (i, 0)),
            pl.BlockSpec((TOP_K, tm), lambda i: (0, i)),
            pl.BlockSpec((TOP_K, tm), lambda i: (0, i)),
            pl.BlockSpec((TOP_K, tm), lambda i: (0, i)),
            pl.BlockSpec((ne, 1), lambda i: (0, 0)),
        ],
        out_shape=[
            jax.ShapeDtypeStruct((n, d // 2), jnp.uint32),
            jax.ShapeDtypeStruct((TOP_K, n), jnp.int32),
            jax.ShapeDtypeStruct((TOP_K, n), F32),
            jax.ShapeDtypeStruct((TOP_K, n), jnp.int32),
            jax.ShapeDtypeStruct((ne, 1), jnp.int32),
        ],
        scratch_shapes=[pltpu.VMEM((ne, 1), F32)],
        compiler_params=_params(("arbitrary",), 48),
        name="router",
    )(x1, ffn_norm, wr_t, b_router)


def _dispatch_kernel(slot_ref, pad0_ref, padn_ref, nused_ref, h_ref, xb_ref, zero_sc, sem, *, tt, n, ne, tm, nblk):
    i = pl.program_id(0)
    zr = zero_sc.shape[0]

    def row_copy(src, r, s):
        return pltpu.make_async_copy(src.at[pl.ds(r, 1)], xb_ref.at[pl.ds(s, 1)], sem)

    def issue(r, carry):
        for k in range(TOP_K):
            row_copy(h_ref, r, slot_ref[k * n + i * tt + r]).start()
        return carry

    def drain(r, carry):
        for k in range(TOP_K):
            row_copy(h_ref, 0, 0).wait()
        return carry

    lax.fori_loop(0, tt, issue, 0)
    lax.fori_loop(0, tt, drain, 0)

    @pl.when(i == pl.num_programs(0) - 1)
    def _():
        zero_sc[...] = jnp.zeros(zero_sc.shape, zero_sc.dtype)

        def per_expert(e, carry):
            start = pad0_ref[e]
            cnt = padn_ref[e]
            lax.fori_loop(0, cnt, lambda r, c: (row_copy(zero_sc, 0, start + r).start(), c)[1], 0)
            lax.fori_loop(0, cnt, lambda r, c: (row_copy(zero_sc, 0, 0).wait(), c)[1], 0)
            return carry

        lax.fori_loop(0, ne, per_expert, 0)

        def strip_copy(blk, c):
            return pltpu.make_async_copy(zero_sc, xb_ref.at[pl.ds(blk * tm + c * zr, zr)], sem)

        def per_block(blk, carry):
            for c in range(tm // zr):
                strip_copy(blk, c).start()
            for c in range(tm // zr):
                strip_copy(blk, c).wait()
            return carry

        lax.fori_loop(nused_ref[0], nblk, per_block, 0)


def _dispatch(slot_flat, pad_start, pad_count, n_used, hp, *, tm, nblk, tt):
    n, half = hp.shape
    ne = pad_start.shape[0]
    zr = _tile(tm, 64, 8)
    kern = functools.partial(_dispatch_kernel, tt=tt, n=n, ne=ne, tm=tm, nblk=nblk)
    return pl.pallas_call(
        kern,
        grid_spec=pltpu.PrefetchScalarGridSpec(
            num_scalar_prefetch=4,
            grid=(n // tt,),
            in_specs=[pl.BlockSpec((tt, half), lambda i, *_: (i, 0))],
            out_specs=pl.BlockSpec(memory_space=pl.ANY),
            scratch_shapes=[pltpu.VMEM((zr, half), jnp.uint32), pltpu.SemaphoreType.DMA(())],
        ),
        out_shape=jax.ShapeDtypeStruct((nblk * tm, half), jnp.uint32),
        compiler_params=_params(("arbitrary",), 48),
        name="dispatch",
    )(slot_flat, pad_start, pad_count, n_used, hp)


def _unpack_rows(p):
    lo = lax.bitcast_convert_type(p << 16, F32).astype(BF16)
    hi = lax.bitcast_convert_type(p & jnp.uint32(0xFFFF0000), F32).astype(BF16)
    return lo, hi


def _new_expert(be_ref, i):
    return jnp.logical_or(i == 0, be_ref[i] != be_ref[jnp.maximum(i - 1, 0)])


def _gateup_kernel(be_ref, nused_ref, x_ref, wg_ref, wu_ref, bg_ref, bu_ref, h1_ref, wg_sc, wu_sc):
    i = pl.program_id(1)

    @pl.when(_new_expert(be_ref, i))
    def _():
        wg_sc[...] = wg_ref[0].astype(BF16)
        wu_sc[...] = wu_ref[0].astype(BF16)

    @pl.when(i < nused_ref[0])
    def _():
        lo, hi = _unpack_rows(x_ref[...])
        half = lo.shape[1]

        def proj(w_sc, b_ref):
            return (jnp.dot(lo, w_sc[:half, :], preferred_element_type=F32)
                    + jnp.dot(hi, w_sc[half:, :], preferred_element_type=F32) + b_ref[0])

        g = jnp.minimum(proj(wg_sc, bg_ref), SWIGLU_LIMIT)
        up = jnp.clip(proj(wu_sc, bu_ref), -SWIGLU_LIMIT, SWIGLU_LIMIT)
        h1_ref[...] = (g * jax.nn.sigmoid(SWIGLU_ALPHA * g) * (up + 1.0)).astype(h1_ref.dtype)

    @pl.when(i >= nused_ref[0])
    def _():
        h1_ref[...] = jnp.zeros(h1_ref.shape, h1_ref.dtype)


def _gateup(block_expert, n_used, xb, w_gate, w_up, b_gate, b_up, *, tm, tn):
    rows, half = xb.shape
    ne, d, dff = w_gate.shape
    nblk = rows // tm

    def xmap(j, i, be, nu):
        return (jnp.minimum(i, nu[0] - 1), 0)

    def wmap(j, i, be, nu):
        return (be[i], 0, j)

    return pl.pallas_call(
        _gateup_kernel,
        grid_spec=pltpu.PrefetchScalarGridSpec(
            num_scalar_prefetch=2,
            grid=(dff // tn, nblk),
            in_specs=[
                pl.BlockSpec((tm, half), xmap),
                pl.BlockSpec((1, d, tn), wmap),
                pl.BlockSpec((1, d, tn), wmap),
                pl.BlockSpec((1, 1, tn), wmap),
                pl.BlockSpec((1, 1, tn), wmap),
            ],
            out_specs=pl.BlockSpec((tm, tn), lambda j, i, be, nu: (i, j)),
            scratch_shapes=[pltpu.VMEM((d, tn), BF16), pltpu.VMEM((d, tn), BF16)],
        ),
        out_shape=jax.ShapeDtypeStruct((rows, dff), BF16),
        compiler_params=_params(("arbitrary", "arbitrary"), 56),
        name="moe_gateup",
    )(block_expert, n_used, xb, w_gate, w_up, b_gate.reshape(ne, 1, dff), b_up.reshape(ne, 1, dff))


def _down_kernel(be_ref, nused_ref, h1_ref, wd_ref, bd_ref, y_ref, wd_sc):
    i = pl.program_id(1)

    @pl.when(_new_expert(be_ref, i))
    def _():
        wd_sc[...] = wd_ref[0].astype(BF16)

    @pl.when(i < nused_ref[0])
    def _():
        y_ref[...] = jnp.dot(h1_ref[...], wd_sc[...], preferred_element_type=F32) + bd_ref[0]

    @pl.when(i >= nused_ref[0])
    def _():
        y_ref[...] = jnp.zeros(y_ref.shape, y_ref.dtype)


def _down(block_expert, n_used, h1, w_down, b_down, *, tm, tn):
    rows, dff = h1.shape
    ne, _, d = w_down.shape
    nblk = rows // tm

    def wmap(j, i, be, nu):
        return (be[i], 0, j)

    return pl.pallas_call(
        _down_kernel,
        grid_spec=pltpu.PrefetchScalarGridSpec(
            num_scalar_prefetch=2,
            grid=(d // tn, nblk),
            in_specs=[
                pl.BlockSpec((tm, dff), lambda j, i, be, nu: (jnp.minimum(i, nu[0] - 1), 0)),
                pl.BlockSpec((1, dff, tn), wmap),
                pl.BlockSpec((1, 1, tn), wmap),
            ],
            out_specs=pl.BlockSpec((tm, tn), lambda j, i, be, nu: (i, j)),
            scratch_shapes=[pltpu.VMEM((dff, tn), BF16)],
        ),
        out_shape=jax.ShapeDtypeStruct((rows, d), F32),
        compiler_params=_params(("arbitrary", "arbitrary"), 56),
        name="moe_down",
    )(block_expert, n_used, h1, w_down, b_down.reshape(ne, 1, d))


def _combine_kernel(slot_ref, x_ref, gate_ref, yb_ref, o_ref, buf_sc, sem, *, tc, n):
    i = pl.program_id(0)

    def row_copy(k, r, s):
        return pltpu.make_async_copy(yb_ref.at[pl.ds(s, 1)], buf_sc.at[k, pl.ds(r, 1)], sem)

    def issue(r, carry):
        for k in range(TOP_K):
            row_copy(k, r, slot_ref[k * n + i * tc + r]).start()
        return carry

    def drain(r, carry):
        for k in range(TOP_K):
            row_copy(k, 0, 0).wait()
        return carry

    lax.fori_loop(0, tc, issue, 0)
    lax.fori_loop(0, tc, drain, 0)

    gates = gate_ref[...]
    acc = x_ref[...]
    for k in range(TOP_K):
        acc = acc + gates[:, k:k + 1] * buf_sc[k]
    o_ref[...] = acc


def _combine(slot_flat, x1, gates, yb, *, tc):
    n, d = x1.shape
    kern = functools.partial(_combine_kernel, tc=tc, n=n)
    return pl.pallas_call(
        kern,
        grid_spec=pltpu.PrefetchScalarGridSpec(
            num_scalar_prefetch=1,
            grid=(n // tc,),
            in_specs=[
                pl.BlockSpec((tc, d), lambda i, *_: (i, 0)),
                pl.BlockSpec((tc, TOP_K), lambda i, *_: (i, 0)),
                pl.BlockSpec(memory_space=pl.ANY),
            ],
            out_specs=pl.BlockSpec((tc, d), lambda i, *_: (i, 0)),
            scratch_shapes=[pltpu.VMEM((TOP_K, tc, d), F32), pltpu.SemaphoreType.DMA(())],
        ),
        out_shape=jax.ShapeDtypeStruct((n, d), F32),
        compiler_params=_params(("arbitrary",), 48),
        name="combine",
    )(slot_flat, x1, gates, yb)


def _rotate_half_cols(w):
    half = w.shape[-1] // 2
    return jnp.concatenate([-w[..., half:], w[..., :half]], axis=-1)


def _rope_tables(pos):
    inv = 1.0 / (ROPE_THETA ** (jnp.arange(0, QK_ROPE, 2, dtype=F32) / QK_ROPE))
    ang = pos.astype(F32)[:, None] * inv[None, :]
    cos, sin = jnp.cos(ang), jnp.sin(ang)
    return jnp.concatenate([cos, cos], axis=-1), jnp.concatenate([sin, sin], axis=-1)


def kernel(x_prompt, x_sample, cache_ckv, cache_kpe, state_pool, attn_norm, w_in, q_a_norm, w_q_b, kv_a_norm, w_kv_b, q_norm, k_norm, w_pool, pool_scale, w_out, ffn_norm, w_router, b_router, w_gate, b_gate, w_up, b_up, w_down, b_down):
    bp, seq, d = x_prompt.shape
    nb, t_dec, _ = x_sample.shape
    past = cache_ckv.shape[1]
    kv_lora = cache_ckv.shape[2]
    d_pool = state_pool.shape[2]
    q_lora = q_a_norm.shape[0]
    nh = w_q_b.shape[1] // QK_HEAD
    ne = w_router.shape[1]
    assert bp == 1 and seq >= POOL_STATE and t_dec >= POOL_STATE
    assert w_in.shape[1] == d_pool + q_lora + kv_lora + QK_ROPE

    n_p, n_s = bp * seq, nb * t_dec
    n = n_p + n_s
    tm = _tile(n, 512)
    x = jnp.concatenate([x_prompt.reshape(n_p, d), x_sample.reshape(n_s, d)], axis=0)

    cos_p, sin_p = _rope_tables(jnp.arange(seq))
    cos_s, sin_s = _rope_tables(past + jnp.arange(t_dec))
    cos2 = jnp.concatenate([cos_p] * bp + [cos_s] * nb, axis=0)
    sin2 = jnp.concatenate([sin_p] * bp + [sin_s] * nb, axis=0)

    tn_in = _tile(kv_lora, 512, 128)
    while d_pool % tn_in or q_lora % tn_in:
        tn_in //= 2
    assert tn_in >= 2 * QK_ROPE
    w_kpe = w_in[:, d_pool + q_lora + kv_lora:]
    w_all = jnp.concatenate(
        [w_in[:, :d_pool + q_lora + kv_lora], w_kpe, _rotate_half_cols(w_kpe),
         jnp.zeros((d, tn_in - 2 * QK_ROPE), w_in.dtype)], axis=1).astype(BF16)
    wq = w_q_b.reshape(q_lora, nh, QK_HEAD)
    wq = jnp.concatenate([wq, _rotate_half_cols(wq[..., QK_NOPE:])], axis=-1)
    wq = wq.transpose(1, 0, 2).astype(BF16)
    wkv = w_kv_b.reshape(kv_lora, nh, QK_NOPE + V_HEAD).transpose(1, 0, 2).astype(BF16)
    w_out_b = w_out.astype(BF16)

    u, cqn, ckv, kpe = _inproj(x, attn_norm.reshape(1, d), w_all, q_a_norm.reshape(1, q_lora),
                               kv_a_norm.reshape(1, kv_lora), cos2, sin2,
                               d_pool=d_pool, q_lora=q_lora, kv_lora=kv_lora, tn=tn_in, tm=tm)
    hb = 4 if nh % 4 == 0 else (2 if nh % 2 == 0 else 1)
    q = _qproj(cqn, wq, q_norm.reshape(1, QK_HEAD), cos2, sin2, tm=tm, hb=hb)
    k_new, v_new = _kvexp(ckv, kpe, wkv, k_norm.reshape(1, QK_HEAD), tm=tm, hb=hb)
    n_c = nb * past
    k_past, v_past = _kvexp(cache_ckv.reshape(n_c, kv_lora), cache_kpe.reshape(n_c, QK_ROPE), wkv,
                            k_norm.reshape(1, QK_HEAD), tm=_tile(n_c, 512), hb=hb)

    attn_p = _attn_prompt(q, k_new, v_new, seq=seq, tq=_tile(seq, 512, CHUNK))
    attn_s = _attn_sample(q, k_new, v_new, k_past, v_past, nb=nb, t=t_dec, past=past, row0=n_p)
    attn = jnp.concatenate([attn_p, attn_s], axis=0)

    u_p = u[:n_p].reshape(bp, seq, d_pool)
    u_s = u[n_p:].reshape(nb, t_dec, d_pool)
    w_pool_b = w_pool.astype(BF16)
    scale2 = pool_scale.reshape(1, d_pool)
    halo_p = jnp.zeros((bp, POOL_HALO, d_pool), F32)
    halo_s = jnp.concatenate([jnp.zeros((nb, 1, d_pool), F32), state_pool], axis=1)
    pool_p = _pool(u_p, halo_p, w_pool_b, scale2, pos0=0, tm=_tile(seq, 512))
    pool_s = _pool(u_s, halo_s, w_pool_b, scale2, pos0=past, tm=t_dec)
    pool = jnp.concatenate([pool_p, pool_s], axis=0)

    x1 = _outproj(pool, attn, w_out_b[:d_pool], w_out_b[d_pool:], x, tm=tm, tn=_tile(d, 1024, 128))

    hp, top_idx, gates, rank, counts = _router(x1, ffn_norm.reshape(1, d), w_router.T.astype(BF16),
                                               b_router.reshape(ne, 1), tm=tm)
    tm_moe = 512
    counts = counts.reshape(ne)
    padded = (counts + tm_moe - 1) // tm_moe * tm_moe
    pend = jnp.cumsum(padded)
    pstart = pend - padded
    slot_flat = (pstart[top_idx] + rank).reshape(TOP_K * n).astype(jnp.int32)
    n_blocks = -(-(n * TOP_K) // tm_moe) + ne
    block_expert = jnp.minimum(
        jnp.searchsorted(pend, jnp.arange(n_blocks, dtype=jnp.int32) * tm_moe, side="right"), ne - 1
    ).astype(jnp.int32)
    n_used = (pend[-1:] // tm_moe).astype(jnp.int32)

    xb = _dispatch(slot_flat, (pstart + counts).astype(jnp.int32), (padded - counts).astype(jnp.int32), n_used, hp,
                   tm=tm_moe, nblk=n_blocks, tt=_tile(n, 256, 8))
    dff = w_gate.shape[2]
    h1 = _gateup(block_expert, n_used, xb, w_gate, w_up, b_gate, b_up, tm=tm_moe, tn=_tile(dff, 256, 128))
    yb = _down(block_expert, n_used, h1, w_down, b_down, tm=tm_moe, tn=_tile(d, 512, 128))
    y = _combine(slot_flat, x1, gates.T, yb, tc=_tile(n, 128, 8))

    return (
        y[:n_p].reshape(bp, seq, d),
        y[n_p:].reshape(nb, t_dec, d),
        ckv[:n_p].reshape(bp, seq, kv_lora),
        kpe[:n_p].reshape(bp, seq, QK_ROPE),
        u_p[:, seq - POOL_STATE:],
        ckv[n_p:].reshape(nb, t_dec, kv_lora),
        kpe[n_p:].reshape(nb, t_dec, QK_ROPE),
        u_s[:, t_dec - POOL_STATE:],
    )
```

```python
import functools
import math

import jax
import jax.numpy as jnp
from jax import lax
from jax.experimental import pallas as pl
from jax.experimental.pallas import tpu as pltpu

CHUNK = 64
POOL_WINDOWS = (2, 4, 8, 16)
POOL_STATE = max(POOL_WINDOWS) - 1
POOL_HALO = POOL_STATE + 1
QK_NOPE = 128
QK_ROPE = 64
QK_HEAD = QK_NOPE + QK_ROPE
V_HEAD = 128
ROPE_THETA = 10000.0
TOP_K = 4
SWIGLU_LIMIT = 7.0
SWIGLU_ALPHA = 1.702
EPS = 1e-6
LOG2_E = 1.4426950408889634

F32 = jnp.float32
BF16 = jnp.bfloat16
MIB = 1024 * 1024
NT_DIMS = (((1,), (1,)), ((), ()))


def _params(semantics, vmem_mib=None):
    kw = dict(dimension_semantics=semantics)
    if vmem_mib is not None:
        kw["vmem_limit_bytes"] = vmem_mib * MIB
    return pltpu.CompilerParams(**kw)


def _tile(n, pref, mult=16):
    t = min(pref, n)
    t -= t % mult
    while t > mult and n % t:
        t -= mult
    assert t >= mult and n % t == 0, (n, pref, mult)
    return t


def _inproj_kernel(x_ref, g_ref, w_ref, qg_ref, kvg_ref, cos_ref, sin_ref,
                   u_ref, cqn_ref, ckv_ref, kpe_ref, h_sc, seg_sc, *, nu, nq, nkv, tn):
    j = pl.program_id(1)

    @pl.when(j == 0)
    def _():
        x = x_ref[...]
        ms = jnp.mean(x * x, axis=-1, keepdims=True)
        h_sc[...] = (x * lax.rsqrt(ms + EPS) * g_ref[...]).astype(BF16)

    y = jnp.dot(h_sc[...], w_ref[...], preferred_element_type=F32)

    @pl.when(j < nu)
    def _():
        u_ref[...] = y

    def finish(n, gain_ref, out_ref):
        parts = [seg_sc[t] for t in range(n - 1)] + [y]
        ss = jnp.sum(parts[0] * parts[0], axis=-1, keepdims=True)
        for p in parts[1:]:
            ss = ss + jnp.sum(p * p, axis=-1, keepdims=True)
        r = lax.rsqrt(ss / (n * tn) + EPS)
        for t, p in enumerate(parts):
            out_ref[:, t * tn:(t + 1) * tn] = (p * r * gain_ref[:, t * tn:(t + 1) * tn]).astype(out_ref.dtype)

    def segment(first, n, gain_ref, out_ref):
        for t in range(n - 1):
            @pl.when(j == first + t)
            def _(t=t):
                seg_sc[t] = y

        @pl.when(j == first + n - 1)
        def _():
            finish(n, gain_ref, out_ref)

    segment(nu, nq, qg_ref, cqn_ref)
    segment(nu + nq, nkv, kvg_ref, ckv_ref)

    @pl.when(j == nu + nq + nkv)
    def _():
        kpe_ref[...] = y[:, :QK_ROPE] * cos_ref[...] + y[:, QK_ROPE:2 * QK_ROPE] * sin_ref[...]


def _inproj(x, attn_norm, w_all, q_a_norm, kv_a_norm, cos2, sin2, *, d_pool, q_lora, kv_lora, tn, tm):
    n, d = x.shape
    nu, nq, nkv = d_pool // tn, q_lora // tn, kv_lora // tn
    nj = nu + nq + nkv + 1
    assert w_all.shape == (d, nj * tn)
    kern = functools.partial(_inproj_kernel, nu=nu, nq=nq, nkv=nkv, tn=tn)
    return pl.pallas_call(
        kern,
        grid=(n // tm, nj),
        in_specs=[
            pl.BlockSpec((tm, d), lambda i, j: (i, 0)),
            pl.BlockSpec((1, d), lambda i, j: (0, 0)),
            pl.BlockSpec((d, tn), lambda i, j: (0, j)),
            pl.BlockSpec((1, q_lora), lambda i, j: (0, 0)),
            pl.BlockSpec((1, kv_lora), lambda i, j: (0, 0)),
            pl.BlockSpec((tm, QK_ROPE), lambda i, j: (i, 0)),
            pl.BlockSpec((tm, QK_ROPE), lambda i, j: (i, 0)),
        ],
        out_specs=[
            pl.BlockSpec((tm, tn), lambda i, j: (i, jnp.minimum(j, nu - 1))),
            pl.BlockSpec((tm, q_lora), lambda i, j: (i, 0)),
            pl.BlockSpec((tm, kv_lora), lambda i, j: (i, 0)),
            pl.BlockSpec((tm, QK_ROPE), lambda i, j: (i, 0)),
        ],
        out_shape=[
            jax.ShapeDtypeStruct((n, d_pool), F32),
            jax.ShapeDtypeStruct((n, q_lora), BF16),
            jax.ShapeDtypeStruct((n, kv_lora), F32),
            jax.ShapeDtypeStruct((n, QK_ROPE), F32),
        ],
        scratch_shapes=[
            pltpu.VMEM((tm, d), BF16),
            pltpu.VMEM((max(nq, nkv, 2) - 1, tm, tn), F32),
        ],
        compiler_params=_params(("parallel", "arbitrary"), 48),
        name="inproj",
    )(x, attn_norm, w_all, q_a_norm, kv_a_norm, cos2, sin2)


def _qproj_kernel(c_ref, w_ref, g_ref, cos_ref, sin_ref, q_ref, *, hb, scale):
    c = c_ref[...]
    cos = cos_ref[...]
    sin = sin_ref[...]
    g = g_ref[...]
    for h in range(hb):
        y = jnp.dot(c, w_ref[h], preferred_element_type=F32)
        nope = y[:, :QK_NOPE]
        rope = y[:, QK_NOPE:QK_HEAD] * cos + y[:, QK_HEAD:QK_HEAD + QK_ROPE] * sin
        ss = jnp.sum(nope * nope, axis=-1, keepdims=True) + jnp.sum(rope * rope, axis=-1, keepdims=True)
        r = lax.rsqrt(ss / QK_HEAD + EPS) * scale
        q_ref[h, :, :QK_NOPE] = (nope * r * g[:, :QK_NOPE]).astype(BF16)
        q_ref[h, :, QK_NOPE:] = (rope * r * g[:, QK_NOPE:]).astype(BF16)


def _qproj(cqn, wq, q_norm, cos2, sin2, *, tm, hb):
    n, q_lora = cqn.shape
    nh = wq.shape[0]
    kern = functools.partial(_qproj_kernel, hb=hb, scale=QK_HEAD ** -0.5 * LOG2_E)
    return pl.pallas_call(
        kern,
        grid=(n // tm, nh // hb),
        in_specs=[
            pl.BlockSpec((tm, q_lora), lambda i, j: (i, 0)),
            pl.BlockSpec((hb, q_lora, QK_HEAD + QK_ROPE), lambda i, j: (j, 0, 0)),
            pl.BlockSpec((1, QK_HEAD), lambda i, j: (0, 0)),
            pl.BlockSpec((tm, QK_ROPE), lambda i, j: (i, 0)),
            pl.BlockSpec((tm, QK_ROPE), lambda i, j: (i, 0)),
        ],
        out_specs=pl.BlockSpec((hb, tm, QK_HEAD), lambda i, j: (j, i, 0)),
        out_shape=jax.ShapeDtypeStruct((nh, n, QK_HEAD), BF16),
        compiler_params=_params(("parallel", "arbitrary"), 48),
        name="qproj",
    )(cqn, wq, q_norm, cos2, sin2)


def _kvexp_kernel(c_ref, pe_ref, w_ref, g_ref, k_ref, v_ref, vt_ref, *, hb):
    c = c_ref[...].astype(BF16)
    pe = pe_ref[...]
    g = g_ref[...]
    pe_ss = jnp.sum(pe * pe, axis=-1, keepdims=True)
    for h in range(hb):
        y = jnp.dot(c, w_ref[h], preferred_element_type=F32)
        nope = y[:, :QK_NOPE]
        r = lax.rsqrt((jnp.sum(nope * nope, axis=-1, keepdims=True) + pe_ss) / QK_HEAD + EPS)
        k_ref[h, :, :QK_NOPE] = (nope * r * g[:, :QK_NOPE]).astype(BF16)
        k_ref[h, :, QK_NOPE:] = (pe * r * g[:, QK_NOPE:]).astype(BF16)
        v = y[:, QK_NOPE:]
        v_ref[h] = v.astype(BF16)
        vt_ref[h, 0] = v.T.astype(BF16)


def _kvexp(ckv, kpe, wkv, k_norm, *, tm, hb):
    n, kv_lora = ckv.shape
    nh = wkv.shape[0]
    kern = functools.partial(_kvexp_kernel, hb=hb)
    return pl.pallas_call(
        kern,
        grid=(n // tm, nh // hb),
        in_specs=[
            pl.BlockSpec((tm, kv_lora), lambda i, j: (i, 0)),
            pl.BlockSpec((tm, QK_ROPE), lambda i, j: (i, 0)),
            pl.BlockSpec((hb, kv_lora, QK_NOPE + V_HEAD), lambda i, j: (j, 0, 0)),
            pl.BlockSpec((1, QK_HEAD), lambda i, j: (0, 0)),
        ],
        out_specs=[
            pl.BlockSpec((hb, tm, QK_HEAD), lambda i, j: (j, i, 0)),
            pl.BlockSpec((hb, tm, V_HEAD), lambda i, j: (j, i, 0)),
            pl.BlockSpec((hb, 1, V_HEAD, tm), lambda i, j: (j, i, 0, 0)),
        ],
        out_shape=[
            jax.ShapeDtypeStruct((nh, n, QK_HEAD), BF16),
            jax.ShapeDtypeStruct((nh, n, V_HEAD), BF16),
            jax.ShapeDtypeStruct((nh, n // tm, V_HEAD, tm), BF16),
        ],
        compiler_params=_params(("parallel", "arbitrary"), 48),
        name="kvexp",
    )(ckv, kpe, wkv, k_norm)


def _attn_prompt_kernel(q_ref, k_ref, vt_ref, o_ref, st_sc, m_sc, l_sc, acc_sc, *, tq):
    i = pl.program_id(1)
    q = q_ref[0]
    m_sc[...] = jnp.full(m_sc.shape, -jnp.inf, F32)
    l_sc[...] = jnp.zeros(l_sc.shape, F32)
    acc_sc[...] = jnp.zeros(acc_sc.shape, F32)

    def qk(c, slot):
        off = pl.multiple_of(c * tq, tq)
        st_sc[slot] = lax.dot_general(k_ref[0, pl.ds(off, tq), :], q, NT_DIMS, preferred_element_type=F32)

    def update(c, slot, diagonal):
        st = st_sc[slot]
        if diagonal:
            key_chunk = lax.broadcasted_iota(jnp.int32, st.shape, 0) // CHUNK
            qry_chunk = lax.broadcasted_iota(jnp.int32, st.shape, 1) // CHUNK
            st = jnp.where(key_chunk <= qry_chunk, st, -jnp.inf)
        m_old = m_sc[...]
        m_new = jnp.maximum(m_old, jnp.max(st, axis=0, keepdims=True))
        alpha = jnp.exp2(m_old - m_new)
        p = jnp.exp2(st - m_new)
        l_sc[...] = alpha * l_sc[...] + jnp.sum(p, axis=0, keepdims=True)
        acc_sc[...] = alpha * acc_sc[...] + jnp.dot(vt_ref[0, c], p.astype(BF16), preferred_element_type=F32)
        m_sc[...] = m_new

    qk(0, 0)

    def pair(j, carry):
        c = 2 * j
        qk(c + 1, 1)
        update(c, 0, False)
        qk(c + 2, 0)
        update(c + 1, 1, False)
        return carry

    lax.fori_loop(0, lax.shift_right_logical(i, 1), pair, 0)

    @pl.when((i & 1) == 0)
    def _():
        update(i, 0, True)

    @pl.when((i & 1) == 1)
    def _():
        qk(i, 1)
        update(i - 1, 0, False)
        update(i, 1, True)

    o_ref[...] = (acc_sc[...] / l_sc[...]).T.astype(o_ref.dtype)


def _attn_prompt(q, k, vt, *, seq, tq):
    nh = q.shape[0]
    assert seq % tq == 0 and tq % CHUNK == 0 and vt.shape[3] == tq
    kern = functools.partial(_attn_prompt_kernel, tq=tq)
    return pl.pallas_call(
        kern,
        grid=(nh, seq // tq),
        in_specs=[
            pl.BlockSpec((1, tq, QK_HEAD), lambda h, i: (h, i, 0)),
            pl.BlockSpec((1, seq, QK_HEAD), lambda h, i: (h, 0, 0)),
            pl.BlockSpec((1, seq // tq, V_HEAD, tq), lambda h, i: (h, 0, 0, 0)),
        ],
        out_specs=pl.BlockSpec((tq, V_HEAD), lambda h, i: (i, h)),
        out_shape=jax.ShapeDtypeStruct((seq, nh * V_HEAD), BF16),
        scratch_shapes=[
            pltpu.VMEM((2, tq, tq), F32),
            pltpu.VMEM((1, tq), F32),
            pltpu.VMEM((1, tq), F32),
            pltpu.VMEM((V_HEAD, tq), F32),
        ],
        compiler_params=_params(("parallel", "arbitrary"), 48),
        name="attn_prompt",
    )(q, k, vt)


def _attn_sample_kernel(q_ref, kp_ref, vp_ref, kn_ref, vn_ref, o_ref):
    q = q_ref[0]
    s1 = lax.dot_general(q, kp_ref[0], NT_DIMS, preferred_element_type=F32)
    s2 = lax.dot_general(q, kn_ref[0], NT_DIMS, preferred_element_type=F32)
    m = jnp.maximum(jnp.max(s1, axis=-1, keepdims=True), jnp.max(s2, axis=-1, keepdims=True))
    p1 = jnp.exp2(s1 - m)
    p2 = jnp.exp2(s2 - m)
    l = jnp.sum(p1, axis=-1, keepdims=True) + jnp.sum(p2, axis=-1, keepdims=True)
    o = (jnp.dot(p1.astype(BF16), vp_ref[0], preferred_element_type=F32)
         + jnp.dot(p2.astype(BF16), vn_ref[0], preferred_element_type=F32))
    o_ref[...] = (o / l).astype(o_ref.dtype)


def _attn_sample(q, k_new, v_new, k_past, v_past, *, nb, t, past, row0):
    nh = q.shape[0]
    assert row0 % t == 0
    r0 = row0 // t
    return pl.pallas_call(
        _attn_sample_kernel,
        grid=(nh, nb),
        in_specs=[
            pl.BlockSpec((1, t, QK_HEAD), lambda h, b: (h, r0 + b, 0)),
            pl.BlockSpec((1, past, QK_HEAD), lambda h, b: (h, b, 0)),
            pl.BlockSpec((1, past, V_HEAD), lambda h, b: (h, b, 0)),
            pl.BlockSpec((1, t, QK_HEAD), lambda h, b: (h, r0 + b, 0)),
            pl.BlockSpec((1, t, V_HEAD), lambda h, b: (h, r0 + b, 0)),
        ],
        out_specs=pl.BlockSpec((t, V_HEAD), lambda h, b: (b, h)),
        out_shape=jax.ShapeDtypeStruct((nb * t, nh * V_HEAD), BF16),
        compiler_params=_params(("parallel", "arbitrary"), 48),
        name="attn_sample",
    )(q, k_past, v_past, k_new, v_new)


def _pool_kernel(u_ref, prev_ref, w_ref, sc_ref, o_ref, ext_sc, *, tm, pg, pos0):
    t = pl.program_id(1)

    @pl.when(t == 0)
    def _():
        ext_sc[0:POOL_HALO, :] = prev_ref[0]

    @pl.when(t > 0)
    def _():
        ext_sc[0:POOL_HALO, :] = ext_sc[tm:tm + POOL_HALO, :]

    u = u_ref[0]
    ext_sc[POOL_HALO:, :] = u
    pos = pos0 + t * tm + lax.broadcasted_iota(jnp.int32, (tm, 1), 0)
    for g, w in enumerate(POOL_WINDOWS):
        cols = slice(g * pg, (g + 1) * pg)
        win = ext_sc[POOL_HALO:POOL_HALO + tm, cols]
        for back in range(1, w):
            win = win + ext_sc[POOL_HALO - back:POOL_HALO - back + tm, cols]
        inv_cnt = 1.0 / jnp.minimum(pos + 1, w).astype(F32)
        pooled = win * inv_cnt - u[:, cols]
        y = jnp.dot(pooled.astype(BF16), w_ref[g], preferred_element_type=F32)
        o_ref[:, cols] = (y * sc_ref[:, cols]).astype(o_ref.dtype)


def _pool(u, prev_halo, w_pool, pool_scale, *, pos0, tm):
    b, t, c = u.shape
    pg = c // len(POOL_WINDOWS)
    nt = t // tm
    kern = functools.partial(_pool_kernel, tm=tm, pg=pg, pos0=pos0)
    return pl.pallas_call(
        kern,
        grid=(b, nt),
        in_specs=[
            pl.BlockSpec((1, tm, c), lambda bi, ti: (bi, ti, 0)),
            pl.BlockSpec((1, POOL_HALO, c), lambda bi, ti: (bi, 0, 0)),
            pl.BlockSpec((len(POOL_WINDOWS), pg, pg), lambda bi, ti: (0, 0, 0)),
            pl.BlockSpec((1, c), lambda bi, ti: (0, 0)),
        ],
        out_specs=pl.BlockSpec((tm, c), lambda bi, ti: (bi * nt + ti, 0)),
        out_shape=jax.ShapeDtypeStruct((b * t, c), BF16),
        scratch_shapes=[pltpu.VMEM((tm + POOL_HALO, c), F32)],
        compiler_params=_params(("arbitrary", "arbitrary"), 48),
        name="pool",
    )(u, prev_halo, w_pool, pool_scale)


def _outproj_kernel(p_ref, a_ref, wp_ref, wa_ref, x_ref, o_ref):
    acc = jnp.dot(p_ref[...], wp_ref[...], preferred_element_type=F32)
    acc = acc + jnp.dot(a_ref[...], wa_ref[...], preferred_element_type=F32)
    o_ref[...] = x_ref[...] + acc


def _outproj(pool, attn, w_p, w_a, x, *, tm, tn):
    n, d = x.shape
    dp, da = pool.shape[1], attn.shape[1]
    return pl.pallas_call(
        _outproj_kernel,
        grid=(d // tn, n // tm),
        in_specs=[
            pl.BlockSpec((tm, dp), lambda j, i: (i, 0)),
            pl.BlockSpec((tm, da), lambda j, i: (i, 0)),
            pl.BlockSpec((dp, tn), lambda j, i: (0, j)),
            pl.BlockSpec((da, tn), lambda j, i: (0, j)),
            pl.BlockSpec((tm, tn), lambda j, i: (i, j)),
        ],
        out_specs=pl.BlockSpec((tm, tn), lambda j, i: (i, j)),
        out_shape=jax.ShapeDtypeStruct((n, d), F32),
        compiler_params=_params(("parallel", "arbitrary"), 48),
        name="outproj",
    )(pool, attn, w_p, w_a, x)


def _router_kernel(x_ref, g_ref, wr_ref, br_ref, hp_ref, idx_ref, gate_ref, rank_ref, cnt_ref, run_sc, *, ne):
    i = pl.program_id(0)

    @pl.when(i == 0)
    def _():
        run_sc[...] = jnp.zeros(run_sc.shape, F32)

    x = x_ref[...]
    tm, d = x.shape
    ms = jnp.mean(x * x, axis=-1, keepdims=True)
    h = x * lax.rsqrt(ms + EPS) * g_ref[...]
    hb = h.astype(BF16)

    bits = lax.bitcast_convert_type(hb.astype(F32), jnp.uint32)
    hp_ref[...] = (bits[:, :d // 2] >> 16) | (bits[:, d // 2:] & jnp.uint32(0xFFFF0000))

    logits = lax.dot_general(wr_ref[...], hb, NT_DIMS, preferred_element_type=F32) + br_ref[...]
    eidx = lax.broadcasted_iota(jnp.int32, logits.shape, 0)
    work = logits
    vals, idxs = [], []
    for _ in range(TOP_K):
        m = jnp.max(work, axis=0, keepdims=True)
        sel = jnp.min(jnp.where(work == m, eidx, ne), axis=0, keepdims=True)
        vals.append(m)
        idxs.append(sel)
        work = jnp.where(eidx == sel, -jnp.inf, work)
    exps = [jnp.exp(v - vals[0]) for v in vals]
    den = exps[0]
    for e in exps[1:]:
        den = den + e

    hits = [(eidx == sel).astype(F32) for sel in idxs]
    onehot = hits[0]
    for hit in hits[1:]:
        onehot = onehot + hit
    before = (lax.broadcasted_iota(jnp.int32, (tm, tm), 0) < lax.broadcasted_iota(jnp.int32, (tm, tm), 1))
    prefix = jnp.dot(onehot.astype(BF16), before.astype(BF16), preferred_element_type=F32)
    base = prefix + run_sc[...]
    for k in range(TOP_K):
        idx_ref[k:k + 1, :] = idxs[k]
        gate_ref[k:k + 1, :] = exps[k] / den
        rank_ref[k:k + 1, :] = jnp.sum(hits[k] * base, axis=0, keepdims=True).astype(jnp.int32)
    run_sc[...] = run_sc[...] + jnp.sum(onehot, axis=1, keepdims=True)
    cnt_ref[...] = run_sc[...].astype(jnp.int32)


def _router(x1, ffn_norm, wr_t, b_router, *, tm):
    n, d = x1.shape
    ne = wr_t.shape[0]
    kern = functools.partial(_router_kernel, ne=ne)
    return pl.pallas_call(
        kern,
        grid=(n // tm,),
        in_specs=[
            pl.BlockSpec((tm, d), lambda i: (i, 0)),
            pl.BlockSpec((1, d), lambda i: (0, 0)),
            pl.BlockSpec((ne, d), lambda i: (0, 0)),
            pl.BlockSpec((ne, 1), lambda i: (0, 0)),
        ],
        out_specs=[
            pl.BlockSpec((tm, d // 2), lambda i: (i, 0)),
            pl.BlockSpec((TOP_K, tm), lambda i: (0, i)),
            pl.BlockSpec((TOP_K, tm), lambda i: (0, i)),
            pl.BlockSpec((TOP_K, tm), lambda i: (0, i)),
            pl.BlockSpec((ne, 1), lambda i: (0, 0)),
        ],
        out_shape=[
            jax.ShapeDtypeStruct((n, d // 2), jnp.uint32),
            jax.ShapeDtypeStruct((TOP_K, n), jnp.int32),
            jax.ShapeDtypeStruct((TOP_K, n), F32),
            jax.ShapeDtypeStruct((TOP_K, n), jnp.int32),
            jax.ShapeDtypeStruct((ne, 1), jnp.int32),
        ],
        scratch_shapes=[pltpu.VMEM((ne, 1), F32)],
        compiler_params=_params(("arbitrary",), 48),
        name="router",
    )(x1, ffn_norm, wr_t, b_router)


def _dispatch_kernel(slot_ref, pad0_ref, padn_ref, nused_ref, h_ref, xb_ref, zero_sc, sem, *, tt, n, ne, tm, nblk):
    i = pl.program_id(0)
    zr = zero_sc.shape[0]

    def row_copy(src, r, s):
        return pltpu.make_async_copy(src.at[pl.ds(r, 1)], xb_ref.at[pl.ds(s, 1)], sem)

    def issue(r, carry):
        for k in range(TOP_K):
            row_copy(h_ref, r, slot_ref[k * n + i * tt + r]).start()
        return carry

    def drain(r, carry):
        for k in range(TOP_K):
            row_copy(h_ref, 0, 0).wait()
        return carry

    lax.fori_loop(0, tt, issue, 0)
    lax.fori_loop(0, tt, drain, 0)

    @pl.when(i == pl.num_programs(0) - 1)
    def _():
        zero_sc[...] = jnp.zeros(zero_sc.shape, zero_sc.dtype)

        def per_expert(e, carry):
            start = pad0_ref[e]
            cnt = padn_ref[e]
            lax.fori_loop(0, cnt, lambda r, c: (row_copy(zero_sc, 0, start + r).start(), c)[1], 0)
            lax.fori_loop(0, cnt, lambda r, c: (row_copy(zero_sc, 0, 0).wait(), c)[1], 0)
            return carry

        lax.fori_loop(0, ne, per_expert, 0)

        def strip_copy(blk, c):
            return pltpu.make_async_copy(zero_sc, xb_ref.at[pl.ds(blk * tm + c * zr, zr)], sem)

        def per_block(blk, carry):
            for c in range(tm // zr):
                strip_copy(blk, c).start()
            for c in range(tm // zr):
                strip_copy(blk, c).wait()
            return carry

        lax.fori_loop(nused_ref[0], nblk, per_block, 0)


def _dispatch(slot_flat, pad_start, pad_count, n_used, hp, *, tm, nblk, tt):
    n, half = hp.shape
    ne = pad_start.shape[0]
    zr = _tile(tm, 64, 8)
    kern = functools.partial(_dispatch_kernel, tt=tt, n=n, ne=ne, tm=tm, nblk=nblk)
    return pl.pallas_call(
        kern,
        grid_spec=pltpu.PrefetchScalarGridSpec(
            num_scalar_prefetch=4,
            grid=(n // tt,),
            in_specs=[pl.BlockSpec((tt, half), lambda i, *_: (i, 0))],
            out_specs=pl.BlockSpec(memory_space=pl.ANY),
            scratch_shapes=[pltpu.VMEM((zr, half), jnp.uint32), pltpu.SemaphoreType.DMA(())],
        ),
        out_shape=jax.ShapeDtypeStruct((nblk * tm, half), jnp.uint32),
        compiler_params=_params(("arbitrary",), 48),
        name="dispatch",
    )(slot_flat, pad_start, pad_count, n_used, hp)


def _unpack_rows(p):
    lo = lax.bitcast_convert_type(p << 16, F32).astype(BF16)
    hi = lax.bitcast_convert_type(p & jnp.uint32(0xFFFF0000), F32).astype(BF16)
    return lo, hi


def _gateup_kernel(be_ref, nused_ref, x_ref, wg_ref, wu_ref, bg_ref, bu_ref, h1_ref):
    i = pl.program_id(1)

    @pl.when(i < nused_ref[0])
    def _():
        lo, hi = _unpack_rows(x_ref[...])
        half = lo.shape[1]

        def proj(w_ref, b_ref):
            return (jnp.dot(lo, w_ref[0, :half, :].astype(BF16), preferred_element_type=F32)
                    + jnp.dot(hi, w_ref[0, half:, :].astype(BF16), preferred_element_type=F32) + b_ref[0])

        g = jnp.minimum(proj(wg_ref, bg_ref), SWIGLU_LIMIT)
        up = jnp.clip(proj(wu_ref, bu_ref), -SWIGLU_LIMIT, SWIGLU_LIMIT)
        h1_ref[...] = (g * jax.nn.sigmoid(SWIGLU_ALPHA * g) * (up + 1.0)).astype(h1_ref.dtype)

    @pl.when(i >= nused_ref[0])
    def _():
        h1_ref[...] = jnp.zeros(h1_ref.shape, h1_ref.dtype)


def _gateup(block_expert, n_used, xb, w_gate, w_up, b_gate, b_up, *, tm, tn):
    rows, half = xb.shape
    ne, d, dff = w_gate.shape
    nblk = rows // tm

    def xmap(j, i, be, nu):
        return (jnp.minimum(i, nu[0] - 1), 0)

    def wmap(j, i, be, nu):
        return (be[i], 0, j)

    return pl.pallas_call(
        _gateup_kernel,
        grid_spec=pltpu.PrefetchScalarGridSpec(
            num_scalar_prefetch=2,
            grid=(dff // tn, nblk),
            in_specs=[
                pl.BlockSpec((tm, half), xmap),
                pl.BlockSpec((1, d, tn), wmap),
                pl.BlockSpec((1, d, tn), wmap),
                pl.BlockSpec((1, 1, tn), wmap),
                pl.BlockSpec((1, 1, tn), wmap),
            ],
            out_specs=pl.BlockSpec((tm, tn), lambda j, i, be, nu: (i, j)),
        ),
        out_shape=jax.ShapeDtypeStruct((rows, dff), BF16),
        compiler_params=_params(("arbitrary", "arbitrary"), 60),
        name="moe_gateup",
    )(block_expert, n_used, xb, w_gate, w_up, b_gate.reshape(ne, 1, dff), b_up.reshape(ne, 1, dff))


def _down_kernel(be_ref, nused_ref, h1_ref, wd_ref, bd_ref, y_ref):
    i = pl.program_id(1)

    @pl.when(i < nused_ref[0])
    def _():
        y_ref[...] = jnp.dot(h1_ref[...], wd_ref[0].astype(BF16), preferred_element_type=F32) + bd_ref[0]

    @pl.when(i >= nused_ref[0])
    def _():
        y_ref[...] = jnp.zeros(y_ref.shape, y_ref.dtype)


def _down(block_expert, n_used, h1, w_down, b_down, *, tm, tn):
    rows, dff = h1.shape
    ne, _, d = w_down.shape
    nblk = rows // tm

    def wmap(j, i, be, nu):
        return (be[i], 0, j)

    return pl.pallas_call(
        _down_kernel,
        grid_spec=pltpu.PrefetchScalarGridSpec(
            num_scalar_prefetch=2,
            grid=(d // tn, nblk),
            in_specs=[
                pl.BlockSpec((tm, dff), lambda j, i, be, nu: (jnp.minimum(i, nu[0] - 1), 0)),
                pl.BlockSpec((1, dff, tn), wmap),
                pl.BlockSpec((1, 1, tn), wmap),
            ],
            out_specs=pl.BlockSpec((tm, tn), lambda j, i, be, nu: (i, j)),
        ),
        out_shape=jax.ShapeDtypeStruct((rows, d), F32),
        compiler_params=_params(("arbitrary", "arbitrary"), 60),
        name="moe_down",
    )(block_expert, n_used, h1, w_down, b_down.reshape(ne, 1, d))


def _combine_kernel(slot_ref, x_ref, gate_ref, yb_ref, o_ref, buf_sc, sem, *, tc, n):
    i = pl.program_id(0)

    def row_copy(k, r, s):
        return pltpu.make_async_copy(yb_ref.at[pl.ds(s, 1)], buf_sc.at[k, pl.ds(r, 1)], sem)

    def issue(r, carry):
        for k in range(TOP_K):
            row_copy(k, r, slot_ref[k * n + i * tc + r]).start()
        return carry

    def drain(r, carry):
        for k in range(TOP_K):
            row_copy(k, 0, 0).wait()
        return carry

    lax.fori_loop(0, tc, issue, 0)
    lax.fori_loop(0, tc, drain, 0)

    gates = gate_ref[...]
    acc = x_ref[...]
    for k in range(TOP_K):
        acc = acc + gates[:, k:k + 1] * buf_sc[k]
    o_ref[...] = acc


def _combine(slot_flat, x1, gates, yb, *, tc):
    n, d = x1.shape
    kern = functools.partial(_combine_kernel, tc=tc, n=n)
    return pl.pallas_call(
        kern,
        grid_spec=pltpu.PrefetchScalarGridSpec(
            num_scalar_prefetch=1,
            grid=(n // tc,),
            in_specs=[
                pl.BlockSpec((tc, d), lambda i, *_: (i, 0)),
                pl.BlockSpec((tc, TOP_K), lambda i, *_: (i, 0)),
                pl.BlockSpec(memory_space=pl.ANY),
            ],
            out_specs=pl.BlockSpec((tc, d), lambda i, *_: (i, 0)),
            scratch_shapes=[pltpu.VMEM((TOP_K, tc, d), F32), pltpu.SemaphoreType.DMA(())],
        ),
        out_shape=jax.ShapeDtypeStruct((n, d), F32),
        compiler_params=_params(("arbitrary",), 48),
        name="combine",
    )(slot_flat, x1, gates, yb)


def _rotate_half_cols(w):
    half = w.shape[-1] // 2
    return jnp.concatenate([-w[..., half:], w[..., :half]], axis=-1)


def _rope_tables(pos):
    inv = 1.0 / (ROPE_THETA ** (jnp.arange(0, QK_ROPE, 2, dtype=F32) / QK_ROPE))
    ang = pos.astype(F32)[:, None] * inv[None, :]
    cos, sin = jnp.cos(ang), jnp.sin(ang)
    return jnp.concatenate([cos, cos], axis=-1), jnp.concatenate([sin, sin], axis=-1)


def kernel(x_prompt, x_sample, cache_ckv, cache_kpe, state_pool, attn_norm, w_in, q_a_norm, w_q_b, kv_a_norm, w_kv_b, q_norm, k_norm, w_pool, pool_scale, w_out, ffn_norm, w_router, b_router, w_gate, b_gate, w_up, b_up, w_down, b_down):
    bp, seq, d = x_prompt.shape
    nb, t_dec, _ = x_sample.shape
    past = cache_ckv.shape[1]
    kv_lora = cache_ckv.shape[2]
    d_pool = state_pool.shape[2]
    q_lora = q_a_norm.shape[0]
    nh = w_q_b.shape[1] // QK_HEAD
    ne = w_router.shape[1]
    assert bp == 1 and seq >= POOL_STATE and t_dec >= POOL_STATE
    assert w_in.shape[1] == d_pool + q_lora + kv_lora + QK_ROPE

    n_p, n_s = bp * seq, nb * t_dec
    n = n_p + n_s
    tm = _tile(n, 512)
    x = jnp.concatenate([x_prompt.reshape(n_p, d), x_sample.reshape(n_s, d)], axis=0)

    cos_p, sin_p = _rope_tables(jnp.arange(seq))
    cos_s, sin_s = _rope_tables(past + jnp.arange(t_dec))
    cos2 = jnp.concatenate([cos_p] * bp + [cos_s] * nb, axis=0)
    sin2 = jnp.concatenate([sin_p] * bp + [sin_s] * nb, axis=0)

    tn_in = _tile(kv_lora, 512, 128)
    while d_pool % tn_in or q_lora % tn_in:
        tn_in //= 2
    assert tn_in >= 2 * QK_ROPE
    w_kpe = w_in[:, d_pool + q_lora + kv_lora:]
    w_all = jnp.concatenate(
        [w_in[:, :d_pool + q_lora + kv_lora], w_kpe, _rotate_half_cols(w_kpe),
         jnp.zeros((d, tn_in - 2 * QK_ROPE), w_in.dtype)], axis=1).astype(BF16)
    wq = w_q_b.reshape(q_lora, nh, QK_HEAD)
    wq = jnp.concatenate([wq, _rotate_half_cols(wq[..., QK_NOPE:])], axis=-1)
    wq = wq.transpose(1, 0, 2).astype(BF16)
    wkv = w_kv_b.reshape(kv_lora, nh, QK_NOPE + V_HEAD).transpose(1, 0, 2).astype(BF16)
    w_out_b = w_out.astype(BF16)

    u, cqn, ckv, kpe = _inproj(x, attn_norm.reshape(1, d), w_all, q_a_norm.reshape(1, q_lora),
                               kv_a_norm.reshape(1, kv_lora), cos2, sin2,
                               d_pool=d_pool, q_lora=q_lora, kv_lora=kv_lora, tn=tn_in, tm=tm)
    hb = 4 if nh % 4 == 0 else (2 if nh % 2 == 0 else 1)
    q = _qproj(cqn, wq, q_norm.reshape(1, QK_HEAD), cos2, sin2, tm=tm, hb=hb)
    t_att = _tile(math.gcd(seq, n), 512, CHUNK)
    k_new, v_new, vt_new = _kvexp(ckv, kpe, wkv, k_norm.reshape(1, QK_HEAD), tm=t_att, hb=hb)
    n_c = nb * past
    k_past, v_past, _ = _kvexp(cache_ckv.reshape(n_c, kv_lora), cache_kpe.reshape(n_c, QK_ROPE), wkv,
                               k_norm.reshape(1, QK_HEAD), tm=_tile(n_c, 512), hb=hb)

    attn_p = _attn_prompt(q, k_new, vt_new, seq=seq, tq=t_att)
    attn_s = _attn_sample(q, k_new, v_new, k_past, v_past, nb=nb, t=t_dec, past=past, row0=n_p)
    attn = jnp.concatenate([attn_p, attn_s], axis=0)

    u_p = u[:n_p].reshape(bp, seq, d_pool)
    u_s = u[n_p:].reshape(nb, t_dec, d_pool)
    w_pool_b = w_pool.astype(BF16)
    scale2 = pool_scale.reshape(1, d_pool)
    halo_p = jnp.zeros((bp, POOL_HALO, d_pool), F32)
    halo_s = jnp.concatenate([jnp.zeros((nb, 1, d_pool), F32), state_pool], axis=1)
    pool_p = _pool(u_p, halo_p, w_pool_b, scale2, pos0=0, tm=_tile(seq, 512))
    pool_s = _pool(u_s, halo_s, w_pool_b, scale2, pos0=past, tm=t_dec)
    pool = jnp.concatenate([pool_p, pool_s], axis=0)

    x1 = _outproj(pool, attn, w_out_b[:d_pool], w_out_b[d_pool:], x, tm=tm, tn=_tile(d, 1024, 128))

    hp, top_idx, gates, rank, counts = _router(x1, ffn_norm.reshape(1, d), w_router.T.astype(BF16),
                                               b_router.reshape(ne, 1), tm=tm)
    tm_moe = 512
    counts = counts.reshape(ne)
    padded = (counts + tm_moe - 1) // tm_moe * tm_moe
    pend = jnp.cumsum(padded)
    pstart = pend - padded
    experts = jnp.arange(ne, dtype=jnp.int32)
    first_slot = jnp.sum(jnp.where(top_idx[..., None] == experts, pstart.astype(jnp.int32), 0), axis=-1)
    slot_flat = (first_slot + rank).reshape(TOP_K * n).astype(jnp.int32)
    n_blocks = -(-(n * TOP_K) // tm_moe) + ne
    block_row0 = jnp.arange(n_blocks, dtype=jnp.int32) * tm_moe
    block_expert = jnp.minimum(jnp.sum(block_row0[:, None] >= pend[None, :], axis=1), ne - 1).astype(jnp.int32)
    n_used = (pend[-1:] // tm_moe).astype(jnp.int32)

    xb = _dispatch(slot_flat, (pstart + counts).astype(jnp.int32), (padded - counts).astype(jnp.int32), n_used, hp,
                   tm=tm_moe, nblk=n_blocks, tt=_tile(n, 256, 8))
    dff = w_gate.shape[2]
    h1 = _gateup(block_expert, n_used, xb, w_gate, w_up, b_gate, b_up, tm=tm_moe, tn=_tile(dff, 512, 128))
    yb = _down(block_expert, n_used, h1, w_down, b_down, tm=tm_moe, tn=_tile(d, 1024, 128))
    y = _combine(slot_flat, x1, gates.T, yb, tc=_tile(n, 128, 8))

    return (
        y[:n_p].reshape(bp, seq, d),
        y[n_p:].reshape(nb, t_dec, d),
        ckv[:n_p].reshape(bp, seq, kv_lora),
        kpe[:n_p].reshape(bp, seq, QK_ROPE),
        u_p[:, seq - POOL_STATE:],
        ckv[n_p:].reshape(nb, t_dec, kv_lora),
        kpe[n_p:].reshape(nb, t_dec, QK_ROPE),
        u_s[:, t_dec - POOL_STATE:],
    )
```

```python
import functools
import math

import jax
import jax.numpy as jnp
from jax import lax
from jax.experimental import pallas as pl
from jax.experimental.pallas import tpu as pltpu

CHUNK = 64
POOL_WINDOWS = (2, 4, 8, 16)
POOL_STATE = max(POOL_WINDOWS) - 1
POOL_HALO = POOL_STATE + 1
QK_NOPE = 128
QK_ROPE = 64
QK_HEAD = QK_NOPE + QK_ROPE
V_HEAD = 128
ROPE_THETA = 10000.0
TOP_K = 4
SWIGLU_LIMIT = 7.0
SWIGLU_ALPHA = 1.702
EPS = 1e-6
LOG2_E = 1.4426950408889634

F32 = jnp.float32
BF16 = jnp.bfloat16
MIB = 1024 * 1024
NT_DIMS = (((1,), (1,)), ((), ()))


def _params(semantics, vmem_mib=None):
    kw = dict(dimension_semantics=semantics)
    if vmem_mib is not None:
        kw["vmem_limit_bytes"] = vmem_mib * MIB
    return pltpu.CompilerParams(**kw)


def _tile(n, pref, mult=16):
    t = min(pref, n)
    t -= t % mult
    while t > mult and n % t:
        t -= mult
    assert t >= mult and n % t == 0, (n, pref, mult)
    return t


def _inproj_kernel(xa_ref, xb_ref, g_ref, w_ref, qg_ref, kvg_ref, cos_ref, sin_ref,
                   u_ref, cqn_ref, ckv_ref, kpe_ref, h_sc, seg_sc, *, nu, nq, nkv, tn, na):
    i = pl.program_id(0)
    j = pl.program_id(1)

    def normalise(x_ref):
        x = x_ref[...]
        ms = jnp.mean(x * x, axis=-1, keepdims=True)
        h_sc[...] = (x * lax.rsqrt(ms + EPS) * g_ref[...]).astype(BF16)

    @pl.when(jnp.logical_and(j == 0, i < na))
    def _():
        normalise(xa_ref)

    @pl.when(jnp.logical_and(j == 0, i >= na))
    def _():
        normalise(xb_ref)

    y = jnp.dot(h_sc[...], w_ref[...], preferred_element_type=F32)

    @pl.when(j < nu)
    def _():
        u_ref[...] = y

    def finish(n, gain_ref, out_ref):
        parts = [seg_sc[t] for t in range(n - 1)] + [y]
        ss = jnp.sum(parts[0] * parts[0], axis=-1, keepdims=True)
        for p in parts[1:]:
            ss = ss + jnp.sum(p * p, axis=-1, keepdims=True)
        r = lax.rsqrt(ss / (n * tn) + EPS)
        for t, p in enumerate(parts):
            out_ref[:, t * tn:(t + 1) * tn] = (p * r * gain_ref[:, t * tn:(t + 1) * tn]).astype(out_ref.dtype)

    def segment(first, n, gain_ref, out_ref):
        for t in range(n - 1):
            @pl.when(j == first + t)
            def _(t=t):
                seg_sc[t] = y

        @pl.when(j == first + n - 1)
        def _():
            finish(n, gain_ref, out_ref)

    segment(nu, nq, qg_ref, cqn_ref)
    segment(nu + nq, nkv, kvg_ref, ckv_ref)

    @pl.when(j == nu + nq + nkv)
    def _():
        kpe_ref[...] = y[:, :QK_ROPE] * cos_ref[...] + y[:, QK_ROPE:2 * QK_ROPE] * sin_ref[...]


def _two_group_maps(na):
    return (lambda i: jnp.minimum(i, na - 1)), (lambda i: jnp.maximum(i - na, 0))


def _inproj(xa, xb, attn_norm, w_all, q_a_norm, kv_a_norm, cos2, sin2, *, d_pool, q_lora, kv_lora, tn, tm):
    d = xa.shape[1]
    assert xa.shape[0] % tm == 0 and xb.shape[0] % tm == 0
    na = xa.shape[0] // tm
    n = xa.shape[0] + xb.shape[0]
    amap, bmap = _two_group_maps(na)
    nu, nq, nkv = d_pool // tn, q_lora // tn, kv_lora // tn
    nj = nu + nq + nkv + 1
    assert w_all.shape == (d, nj * tn)
    kern = functools.partial(_inproj_kernel, nu=nu, nq=nq, nkv=nkv, tn=tn, na=na)
    return pl.pallas_call(
        kern,
        grid=(n // tm, nj),
        in_specs=[
            pl.BlockSpec((tm, d), lambda i, j: (amap(i), 0)),
            pl.BlockSpec((tm, d), lambda i, j: (bmap(i), 0), pipeline_mode=pl.Buffered(1)),
            pl.BlockSpec((1, d), lambda i, j: (0, 0)),
            pl.BlockSpec((d, tn), lambda i, j: (0, j)),
            pl.BlockSpec((1, q_lora), lambda i, j: (0, 0)),
            pl.BlockSpec((1, kv_lora), lambda i, j: (0, 0)),
            pl.BlockSpec((tm, QK_ROPE), lambda i, j: (i, 0)),
            pl.BlockSpec((tm, QK_ROPE), lambda i, j: (i, 0)),
        ],
        out_specs=[
            pl.BlockSpec((tm, tn), lambda i, j: (i, jnp.minimum(j, nu - 1))),
            pl.BlockSpec((tm, q_lora), lambda i, j: (i, 0)),
            pl.BlockSpec((tm, kv_lora), lambda i, j: (i, 0)),
            pl.BlockSpec((tm, QK_ROPE), lambda i, j: (i, 0)),
        ],
        out_shape=[
            jax.ShapeDtypeStruct((n, d_pool), F32),
            jax.ShapeDtypeStruct((n, q_lora), BF16),
            jax.ShapeDtypeStruct((n, kv_lora), F32),
            jax.ShapeDtypeStruct((n, QK_ROPE), F32),
        ],
        scratch_shapes=[
            pltpu.VMEM((tm, d), BF16),
            pltpu.VMEM((max(nq, nkv, 2) - 1, tm, tn), F32),
        ],
        compiler_params=_params(("parallel", "arbitrary"), 58),
        name="inproj",
    )(xa, xb, attn_norm, w_all, q_a_norm, kv_a_norm, cos2, sin2)


def _qproj_kernel(c_ref, w_ref, g_ref, cos_ref, sin_ref, q_ref, *, hb, scale):
    c = c_ref[...]
    cos = cos_ref[...]
    sin = sin_ref[...]
    g = g_ref[...]
    for h in range(hb):
        y = jnp.dot(c, w_ref[h], preferred_element_type=F32)
        nope = y[:, :QK_NOPE]
        rope = y[:, QK_NOPE:QK_HEAD] * cos + y[:, QK_HEAD:QK_HEAD + QK_ROPE] * sin
        ss = jnp.sum(nope * nope, axis=-1, keepdims=True) + jnp.sum(rope * rope, axis=-1, keepdims=True)
        r = lax.rsqrt(ss / QK_HEAD + EPS) * scale
        q_ref[h, :, :QK_NOPE] = (nope * r * g[:, :QK_NOPE]).astype(BF16)
        q_ref[h, :, QK_NOPE:] = (rope * r * g[:, QK_NOPE:]).astype(BF16)


def _qproj(cqn, wq, q_norm, cos2, sin2, *, tm, hb):
    n, q_lora = cqn.shape
    nh = wq.shape[0]
    kern = functools.partial(_qproj_kernel, hb=hb, scale=QK_HEAD ** -0.5 * LOG2_E)
    return pl.pallas_call(
        kern,
        grid=(n // tm, nh // hb),
        in_specs=[
            pl.BlockSpec((tm, q_lora), lambda i, j: (i, 0)),
            pl.BlockSpec((hb, q_lora, QK_HEAD + QK_ROPE), lambda i, j: (j, 0, 0)),
            pl.BlockSpec((1, QK_HEAD), lambda i, j: (0, 0)),
            pl.BlockSpec((tm, QK_ROPE), lambda i, j: (i, 0)),
            pl.BlockSpec((tm, QK_ROPE), lambda i, j: (i, 0)),
        ],
        out_specs=pl.BlockSpec((hb, tm, QK_HEAD), lambda i, j: (j, i, 0)),
        out_shape=jax.ShapeDtypeStruct((nh, n, QK_HEAD), BF16),
        compiler_params=_params(("parallel", "arbitrary"), 48),
        name="qproj",
    )(cqn, wq, q_norm, cos2, sin2)


def _kvexp_kernel(c_ref, pe_ref, w_ref, g_ref, k_ref, v_ref, *maybe_vt_ref, hb):
    c = c_ref[...].astype(BF16)
    pe = pe_ref[...]
    g = g_ref[...]
    pe_ss = jnp.sum(pe * pe, axis=-1, keepdims=True)
    for h in range(hb):
        y = jnp.dot(c, w_ref[h], preferred_element_type=F32)
        nope = y[:, :QK_NOPE]
        r = lax.rsqrt((jnp.sum(nope * nope, axis=-1, keepdims=True) + pe_ss) / QK_HEAD + EPS)
        k_ref[h, :, :QK_NOPE] = (nope * r * g[:, :QK_NOPE]).astype(BF16)
        k_ref[h, :, QK_NOPE:] = (pe * r * g[:, QK_NOPE:]).astype(BF16)
        v = y[:, QK_NOPE:]
        v_ref[h] = v.astype(BF16)
        for vt_ref in maybe_vt_ref:
            vt_ref[h, 0] = v.T.astype(BF16)


def _kvexp(ckv, kpe, wkv, k_norm, *, tm, hb, with_vt):
    n, kv_lora = ckv.shape
    nh = wkv.shape[0]
    kern = functools.partial(_kvexp_kernel, hb=hb)
    vt_spec = [pl.BlockSpec((hb, 1, V_HEAD, tm), lambda i, j: (j, i, 0, 0))] if with_vt else []
    vt_shape = [jax.ShapeDtypeStruct((nh, n // tm, V_HEAD, tm), BF16)] if with_vt else []
    return pl.pallas_call(
        kern,
        grid=(n // tm, nh // hb),
        in_specs=[
            pl.BlockSpec((tm, kv_lora), lambda i, j: (i, 0)),
            pl.BlockSpec((tm, QK_ROPE), lambda i, j: (i, 0)),
            pl.BlockSpec((hb, kv_lora, QK_NOPE + V_HEAD), lambda i, j: (j, 0, 0)),
            pl.BlockSpec((1, QK_HEAD), lambda i, j: (0, 0)),
        ],
        out_specs=[
            pl.BlockSpec((hb, tm, QK_HEAD), lambda i, j: (j, i, 0)),
            pl.BlockSpec((hb, tm, V_HEAD), lambda i, j: (j, i, 0)),
        ] + vt_spec,
        out_shape=[
            jax.ShapeDtypeStruct((nh, n, QK_HEAD), BF16),
            jax.ShapeDtypeStruct((nh, n, V_HEAD), BF16),
        ] + vt_shape,
        compiler_params=_params(("parallel", "arbitrary"), 48),
        name="kvexp",
    )(ckv, kpe, wkv, k_norm)


def _attn_prompt_kernel(q_ref, k_ref, vt_ref, o_ref, st_sc, m_sc, l_sc, acc_sc, *, tk):
    i = pl.program_id(1)
    q = q_ref[0]
    m_sc[...] = jnp.full(m_sc.shape, -jnp.inf, F32)
    l_sc[...] = jnp.zeros(l_sc.shape, F32)
    acc_sc[...] = jnp.zeros(acc_sc.shape, F32)

    def qk(c, slot):
        off = pl.multiple_of(c * tk, tk)
        st_sc[slot] = lax.dot_general(k_ref[0, pl.ds(off, tk), :], q, NT_DIMS, preferred_element_type=F32)

    def update(c, slot, diagonal=None):
        st = st_sc[slot]
        if diagonal is not None:
            key_chunk = (diagonal * tk + lax.broadcasted_iota(jnp.int32, st.shape, 0)) // CHUNK
            qry_chunk = lax.broadcasted_iota(jnp.int32, st.shape, 1) // CHUNK
            st = jnp.where(key_chunk <= qry_chunk, st, -jnp.inf)
        m_old = m_sc[...]
        m_new = jnp.maximum(m_old, jnp.max(st, axis=0, keepdims=True))
        alpha = jnp.exp2(m_old - m_new)
        p = jnp.exp2(st - m_new)
        l_sc[...] = alpha * l_sc[...] + jnp.sum(p, axis=0, keepdims=True)
        acc_sc[...] = alpha * acc_sc[...] + jnp.dot(vt_ref[0, c], p.astype(BF16), preferred_element_type=F32)
        m_sc[...] = m_new

    qk(0, 0)

    def pair(j, carry):
        c = 2 * j
        qk(c + 1, 1)
        update(c, 0)
        qk(c + 2, 0)
        update(c + 1, 1)
        return carry

    lax.fori_loop(0, i, pair, 0)
    qk(2 * i + 1, 1)
    update(2 * i, 0, diagonal=0)
    update(2 * i + 1, 1, diagonal=1)

    o_ref[...] = (acc_sc[...] / l_sc[...]).T.astype(o_ref.dtype)


def _attn_prompt(q, k, vt, *, seq, tk):
    nh = q.shape[0]
    tq = 2 * tk
    assert seq % tq == 0 and tk % CHUNK == 0 and vt.shape[3] == tk
    kern = functools.partial(_attn_prompt_kernel, tk=tk)
    return pl.pallas_call(
        kern,
        grid=(nh, seq // tq),
        in_specs=[
            pl.BlockSpec((1, tq, QK_HEAD), lambda h, i: (h, i, 0)),
            pl.BlockSpec((1, seq, QK_HEAD), lambda h, i: (h, 0, 0)),
            pl.BlockSpec((1, seq // tk, V_HEAD, tk), lambda h, i: (h, 0, 0, 0)),
        ],
        out_specs=pl.BlockSpec((tq, V_HEAD), lambda h, i: (i, h)),
        out_shape=jax.ShapeDtypeStruct((seq, nh * V_HEAD), BF16),
        scratch_shapes=[
            pltpu.VMEM((2, tk, tq), F32),
            pltpu.VMEM((1, tq), F32),
            pltpu.VMEM((1, tq), F32),
            pltpu.VMEM((V_HEAD, tq), F32),
        ],
        compiler_params=_params(("parallel", "arbitrary"), 48),
        name="attn_prompt",
    )(q, k, vt)


def _attn_sample_kernel(q_ref, kp_ref, vp_ref, kn_ref, vn_ref, o_ref, *, hb):
    for h in range(hb):
        q = q_ref[h]
        s1 = lax.dot_general(q, kp_ref[h], NT_DIMS, preferred_element_type=F32)
        s2 = lax.dot_general(q, kn_ref[h], NT_DIMS, preferred_element_type=F32)
        m = jnp.maximum(jnp.max(s1, axis=-1, keepdims=True), jnp.max(s2, axis=-1, keepdims=True))
        p1 = jnp.exp2(s1 - m)
        p2 = jnp.exp2(s2 - m)
        l = jnp.sum(p1, axis=-1, keepdims=True) + jnp.sum(p2, axis=-1, keepdims=True)
        o = (jnp.dot(p1.astype(BF16), vp_ref[h], preferred_element_type=F32)
             + jnp.dot(p2.astype(BF16), vn_ref[h], preferred_element_type=F32))
        o_ref[:, h * V_HEAD:(h + 1) * V_HEAD] = (o / l).astype(o_ref.dtype)


def _attn_sample(q, k_new, v_new, k_past, v_past, *, nb, t, past, row0, hb):
    nh = q.shape[0]
    assert row0 % t == 0 and nh % hb == 0
    r0 = row0 // t
    return pl.pallas_call(
        functools.partial(_attn_sample_kernel, hb=hb),
        grid=(nh // hb, nb),
        in_specs=[
            pl.BlockSpec((hb, t, QK_HEAD), lambda h, b: (h, r0 + b, 0)),
            pl.BlockSpec((hb, past, QK_HEAD), lambda h, b: (h, b, 0)),
            pl.BlockSpec((hb, past, V_HEAD), lambda h, b: (h, b, 0)),
            pl.BlockSpec((hb, t, QK_HEAD), lambda h, b: (h, r0 + b, 0)),
            pl.BlockSpec((hb, t, V_HEAD), lambda h, b: (h, r0 + b, 0)),
        ],
        out_specs=pl.BlockSpec((t, hb * V_HEAD), lambda h, b: (b, h)),
        out_shape=jax.ShapeDtypeStruct((nb * t, nh * V_HEAD), BF16),
        compiler_params=_params(("parallel", "arbitrary"), 48),
        name="attn_sample",
    )(q, k_past, v_past, k_new, v_new)


def _pool_kernel(u_ref, prev_ref, w_ref, sc_ref, o_ref, ext_sc, *, tm, pg, pos0):
    t = pl.program_id(1)

    @pl.when(t == 0)
    def _():
        ext_sc[0:POOL_HALO, :] = prev_ref[0]

    @pl.when(t > 0)
    def _():
        ext_sc[0:POOL_HALO, :] = ext_sc[tm:tm + POOL_HALO, :]

    u = u_ref[0]
    ext_sc[POOL_HALO:, :] = u
    pos = pos0 + t * tm + lax.broadcasted_iota(jnp.int32, (tm, 1), 0)
    for g, w in enumerate(POOL_WINDOWS):
        cols = slice(g * pg, (g + 1) * pg)
        win = ext_sc[POOL_HALO:POOL_HALO + tm, cols]
        for back in range(1, w):
            win = win + ext_sc[POOL_HALO - back:POOL_HALO - back + tm, cols]
        inv_cnt = 1.0 / jnp.minimum(pos + 1, w).astype(F32)
        pooled = win * inv_cnt - u[:, cols]
        y = jnp.dot(pooled.astype(BF16), w_ref[g], preferred_element_type=F32)
        o_ref[:, cols] = (y * sc_ref[:, cols]).astype(o_ref.dtype)


def _pool(u, prev_halo, w_pool, pool_scale, *, pos0, tm):
    b, t, c = u.shape
    pg = c // len(POOL_WINDOWS)
    nt = t // tm
    kern = functools.partial(_pool_kernel, tm=tm, pg=pg, pos0=pos0)
    return pl.pallas_call(
        kern,
        grid=(b, nt),
        in_specs=[
            pl.BlockSpec((1, tm, c), lambda bi, ti: (bi, ti, 0)),
            pl.BlockSpec((1, POOL_HALO, c), lambda bi, ti: (bi, 0, 0)),
            pl.BlockSpec((len(POOL_WINDOWS), pg, pg), lambda bi, ti: (0, 0, 0)),
            pl.BlockSpec((1, c), lambda bi, ti: (0, 0)),
        ],
        out_specs=pl.BlockSpec((tm, c), lambda bi, ti: (bi * nt + ti, 0)),
        out_shape=jax.ShapeDtypeStruct((b * t, c), BF16),
        scratch_shapes=[pltpu.VMEM((tm + POOL_HALO, c), F32)],
        compiler_params=_params(("arbitrary", "arbitrary"), 48),
        name="pool",
    )(u, prev_halo, w_pool, pool_scale)


def _outproj_kernel(pa_ref, pb_ref, aa_ref, ab_ref, wp_ref, wa_ref, xa_ref, xb_ref, o_ref, *, na):
    i = pl.program_id(1)

    def run(p_ref, a_ref, x_ref):
        acc = jnp.dot(p_ref[...], wp_ref[...], preferred_element_type=F32)
        acc = acc + jnp.dot(a_ref[...], wa_ref[...], preferred_element_type=F32)
        o_ref[...] = x_ref[...] + acc

    @pl.when(i < na)
    def _():
        run(pa_ref, aa_ref, xa_ref)

    @pl.when(i >= na)
    def _():
        run(pb_ref, ab_ref, xb_ref)


def _outproj(pool_a, pool_b, attn_a, attn_b, w_p, w_a, xa, xb, *, tm, tn):
    d = xa.shape[1]
    assert xa.shape[0] % tm == 0 and xb.shape[0] % tm == 0
    na = xa.shape[0] // tm
    n = xa.shape[0] + xb.shape[0]
    amap, bmap = _two_group_maps(na)
    dp, da = pool_a.shape[1], attn_a.shape[1]
    return pl.pallas_call(
        functools.partial(_outproj_kernel, na=na),
        grid=(d // tn, n // tm),
        in_specs=[
            pl.BlockSpec((tm, dp), lambda j, i: (amap(i), 0)),
            pl.BlockSpec((tm, dp), lambda j, i: (bmap(i), 0)),
            pl.BlockSpec((tm, da), lambda j, i: (amap(i), 0)),
            pl.BlockSpec((tm, da), lambda j, i: (bmap(i), 0)),
            pl.BlockSpec((dp, tn), lambda j, i: (0, j)),
            pl.BlockSpec((da, tn), lambda j, i: (0, j)),
            pl.BlockSpec((tm, tn), lambda j, i: (amap(i), j)),
            pl.BlockSpec((tm, tn), lambda j, i: (bmap(i), j)),
        ],
        out_specs=pl.BlockSpec((tm, tn), lambda j, i: (i, j)),
        out_shape=jax.ShapeDtypeStruct((n, d), F32),
        compiler_params=_params(("parallel", "arbitrary"), 48),
        name="outproj",
    )(pool_a, pool_b, attn_a, attn_b, w_p, w_a, xa, xb)


def _router_kernel(x_ref, g_ref, wr_ref, br_ref, hp_ref, idx_ref, gate_ref, rank_ref, cnt_ref, run_sc, *, ne):
    i = pl.program_id(0)

    @pl.when(i == 0)
    def _():
        run_sc[...] = jnp.zeros(run_sc.shape, F32)

    x = x_ref[...]
    tm, d = x.shape
    ms = jnp.mean(x * x, axis=-1, keepdims=True)
    h = x * lax.rsqrt(ms + EPS) * g_ref[...]
    hb = h.astype(BF16)

    bits = lax.bitcast_convert_type(hb.astype(F32), jnp.uint32)
    hp_ref[...] = (bits[:, :d // 2] >> 16) | (bits[:, d // 2:] & jnp.uint32(0xFFFF0000))

    logits = lax.dot_general(wr_ref[...], hb, NT_DIMS, preferred_element_type=F32) + br_ref[...]
    eidx = lax.broadcasted_iota(jnp.int32, logits.shape, 0)
    work = logits
    vals, idxs = [], []
    for _ in range(TOP_K):
        m = jnp.max(work, axis=0, keepdims=True)
        sel = jnp.min(jnp.where(work == m, eidx, ne), axis=0, keepdims=True)
        vals.append(m)
        idxs.append(sel)
        work = jnp.where(eidx == sel, -jnp.inf, work)
    exps = [jnp.exp(v - vals[0]) for v in vals]
    den = exps[0]
    for e in exps[1:]:
        den = den + e

    hits = [(eidx == sel).astype(F32) for sel in idxs]
    onehot = hits[0]
    for hit in hits[1:]:
        onehot = onehot + hit
    before = (lax.broadcasted_iota(jnp.int32, (tm, tm), 0) < lax.broadcasted_iota(jnp.int32, (tm, tm), 1))
    prefix = jnp.dot(onehot.astype(BF16), before.astype(BF16), preferred_element_type=F32)
    base = prefix + run_sc[...]
    for k in range(TOP_K):
        idx_ref[k:k + 1, :] = idxs[k]
        gate_ref[k:k + 1, :] = exps[k] / den
        rank_ref[k:k + 1, :] = jnp.sum(hits[k] * base, axis=0, keepdims=True).astype(jnp.int32)
    run_sc[...] = run_sc[...] + jnp.sum(onehot, axis=1, keepdims=True)
    cnt_ref[...] = run_sc[...].astype(jnp.int32)


def _router(x1, ffn_norm, wr_t, b_router, *, tm):
    n, d = x1.shape
    ne = wr_t.shape[0]
    kern = functools.partial(_router_kernel, ne=ne)
    return pl.pallas_call(
        kern,
        grid=(n // tm,),
        in_specs=[
            pl.BlockSpec((tm, d), lambda i: (i, 0)),
            pl.BlockSpec((1, d), lambda i: (0, 0)),
            pl.BlockSpec((ne, d), lambda i: (0, 0)),
            pl.BlockSpec((ne, 1), lambda i: (0, 0)),
        ],
        out_specs=[
            pl.BlockSpec((tm, d // 2), lambda i: (i, 0)),
            pl.BlockSpec((TOP_K, tm), lambda i: (0, i)),
            pl.BlockSpec((TOP_K, tm), lambda i: (0, i)),
            pl.BlockSpec((TOP_K, tm), lambda i: (0, i)),
            pl.BlockSpec((ne, 1), lambda i: (0, 0)),
        ],
        out_shape=[
            jax.ShapeDtypeStruct((n, d // 2), jnp.uint32),
            jax.ShapeDtypeStruct((TOP_K, n), jnp.int32),
            jax.ShapeDtypeStruct((TOP_K, n), F32),
            jax.ShapeDtypeStruct((TOP_K, n), jnp.int32),
            jax.ShapeDtypeStruct((ne, 1), jnp.int32),
        ],
        scratch_shapes=[pltpu.VMEM((ne, 1), F32)],
        compiler_params=_params(("arbitrary",), 48),
        name="router",
    )(x1, ffn_norm, wr_t, b_router)


def _dispatch_kernel(slot_ref, pad0_ref, padn_ref, nused_ref, h_ref, xb_ref, zero_sc, sem, *, tt, n, ne, tm, nblk):
    i = pl.program_id(0)
    zr = zero_sc.shape[0]

    def row_copy(src, r, s):
        return pltpu.make_async_copy(src.at[pl.ds(r, 1)], xb_ref.at[pl.ds(s, 1)], sem)

    def issue(r, carry):
        for k in range(TOP_K):
            row_copy(h_ref, r, slot_ref[k * n + i * tt + r]).start()
        return carry

    def drain(r, carry):
        for k in range(TOP_K):
            row_copy(h_ref, 0, 0).wait()
        return carry

    lax.fori_loop(0, tt, issue, 0)
    lax.fori_loop(0, tt, drain, 0)

    @pl.when(i == pl.num_programs(0) - 1)
    def _():
        zero_sc[...] = jnp.zeros(zero_sc.shape, zero_sc.dtype)

        def per_expert(e, carry):
            start = pad0_ref[e]
            cnt = padn_ref[e]
            lax.fori_loop(0, cnt, lambda r, c: (row_copy(zero_sc, 0, start + r).start(), c)[1], 0)
            lax.fori_loop(0, cnt, lambda r, c: (row_copy(zero_sc, 0, 0).wait(), c)[1], 0)
            return carry

        lax.fori_loop(0, ne, per_expert, 0)

        def strip_copy(blk, c):
            return pltpu.make_async_copy(zero_sc, xb_ref.at[pl.ds(blk * tm + c * zr, zr)], sem)

        def per_block(blk, carry):
            for c in range(tm // zr):
                strip_copy(blk, c).start()
            for c in range(tm // zr):
                strip_copy(blk, c).wait()
            return carry

        lax.fori_loop(nused_ref[0], nblk, per_block, 0)


def _dispatch(slot_flat, pad_start, pad_count, n_used, hp, *, tm, nblk, tt):
    n, half = hp.shape
    ne = pad_start.shape[0]
    zr = _tile(tm, 64, 8)
    kern = functools.partial(_dispatch_kernel, tt=tt, n=n, ne=ne, tm=tm, nblk=nblk)
    return pl.pallas_call(
        kern,
        grid_spec=pltpu.PrefetchScalarGridSpec(
            num_scalar_prefetch=4,
            grid=(n // tt,),
            in_specs=[pl.BlockSpec((tt, half), lambda i, *_: (i, 0))],
            out_specs=pl.BlockSpec(memory_space=pl.ANY),
            scratch_shapes=[pltpu.VMEM((zr, half), jnp.uint32), pltpu.SemaphoreType.DMA(())],
        ),
        out_shape=jax.ShapeDtypeStruct((nblk * tm, half), jnp.uint32),
        compiler_params=_params(("arbitrary",), 48),
        name="dispatch",
    )(slot_flat, pad_start, pad_count, n_used, hp)


def _unpack_rows(p):
    lo = lax.bitcast_convert_type(p << 16, F32).astype(BF16)
    hi = lax.bitcast_convert_type(p & jnp.uint32(0xFFFF0000), F32).astype(BF16)
    return lo, hi


def _gateup_kernel(be_ref, nused_ref, nvalid_ref, x_ref, wg_ref, wu_ref, bg_ref, bu_ref, h1_ref, *, nsub):
    nv = nvalid_ref[pl.program_id(1)]
    sub = x_ref.shape[0] // nsub
    half = x_ref.shape[1]
    for s in range(nsub):
        rows = slice(s * sub, (s + 1) * sub)

        @pl.when(nv > s * sub)
        def _(rows=rows):
            lo, hi = _unpack_rows(x_ref[rows, :])

            def proj(w_ref, b_ref):
                return (jnp.dot(lo, w_ref[0, :half, :].astype(BF16), preferred_element_type=F32)
                        + jnp.dot(hi, w_ref[0, half:, :].astype(BF16), preferred_element_type=F32) + b_ref[0])

            g = jnp.minimum(proj(wg_ref, bg_ref), SWIGLU_LIMIT)
            up = jnp.clip(proj(wu_ref, bu_ref), -SWIGLU_LIMIT, SWIGLU_LIMIT)
            h1_ref[rows, :] = (g * jax.nn.sigmoid(SWIGLU_ALPHA * g) * (up + 1.0)).astype(h1_ref.dtype)

        @pl.when(nv <= s * sub)
        def _(rows=rows):
            h1_ref[rows, :] = jnp.zeros((sub, h1_ref.shape[1]), h1_ref.dtype)


def _gateup(block_expert, n_used, n_valid, xb, w_gate, w_up, b_gate, b_up, *, tm, tn, nsub):
    rows, half = xb.shape
    ne, d, dff = w_gate.shape
    nblk = rows // tm

    def xmap(j, i, be, nu, nv):
        return (jnp.minimum(i, nu[0] - 1), 0)

    def wmap(j, i, be, nu, nv):
        return (be[i], 0, j)

    return pl.pallas_call(
        functools.partial(_gateup_kernel, nsub=nsub),
        grid_spec=pltpu.PrefetchScalarGridSpec(
            num_scalar_prefetch=3,
            grid=(dff // tn, nblk),
            in_specs=[
                pl.BlockSpec((tm, half), xmap),
                pl.BlockSpec((1, d, tn), wmap),
                pl.BlockSpec((1, d, tn), wmap),
                pl.BlockSpec((1, 1, tn), wmap),
                pl.BlockSpec((1, 1, tn), wmap),
            ],
            out_specs=pl.BlockSpec((tm, tn), lambda j, i, be, nu, nv: (i, j)),
        ),
        out_shape=jax.ShapeDtypeStruct((rows, dff), BF16),
        compiler_params=_params(("arbitrary", "arbitrary"), 60),
        name="moe_gateup",
    )(block_expert, n_used, n_valid, xb, w_gate, w_up, b_gate.reshape(ne, 1, dff), b_up.reshape(ne, 1, dff))


def _down_kernel(be_ref, nused_ref, nvalid_ref, h1_ref, wd_ref, bd_ref, y_ref, *, nsub):
    nv = nvalid_ref[pl.program_id(1)]
    sub = h1_ref.shape[0] // nsub
    for s in range(nsub):
        rows = slice(s * sub, (s + 1) * sub)

        @pl.when(nv > s * sub)
        def _(rows=rows):
            y_ref[rows, :] = (jnp.dot(h1_ref[rows, :], wd_ref[0].astype(BF16), preferred_element_type=F32)
                              + bd_ref[0])

        @pl.when(nv <= s * sub)
        def _(rows=rows):
            y_ref[rows, :] = jnp.zeros((sub, y_ref.shape[1]), y_ref.dtype)


def _down(block_expert, n_used, n_valid, h1, w_down, b_down, *, tm, tn, nsub):
    rows, dff = h1.shape
    ne, _, d = w_down.shape
    nblk = rows // tm

    def wmap(j, i, be, nu, nv):
        return (be[i], 0, j)

    return pl.pallas_call(
        functools.partial(_down_kernel, nsub=nsub),
        grid_spec=pltpu.PrefetchScalarGridSpec(
            num_scalar_prefetch=3,
            grid=(d // tn, nblk),
            in_specs=[
                pl.BlockSpec((tm, dff), lambda j, i, be, nu, nv: (jnp.minimum(i, nu[0] - 1), 0)),
                pl.BlockSpec((1, dff, tn), wmap),
                pl.BlockSpec((1, 1, tn), wmap),
            ],
            out_specs=pl.BlockSpec((tm, tn), lambda j, i, be, nu, nv: (i, j)),
        ),
        out_shape=jax.ShapeDtypeStruct((rows, d), F32),
        compiler_params=_params(("arbitrary", "arbitrary"), 60),
        name="moe_down",
    )(block_expert, n_used, n_valid, h1, w_down, b_down.reshape(ne, 1, d))


def _combine_kernel(slot_ref, x_ref, gate_ref, yb_ref, oa_ref, ob_ref, buf_sc, sem, *, tc, n, na):
    i = pl.program_id(0)
    nsteps = pl.num_programs(0)

    def row_copy(buf, k, r, s):
        return pltpu.make_async_copy(yb_ref.at[pl.ds(s, 1)], buf_sc.at[buf, k, pl.ds(r, 1)], sem.at[buf])

    def issue(step, buf):
        def body(r, carry):
            for k in range(TOP_K):
                row_copy(buf, k, r, slot_ref[k * n + step * tc + r]).start()
            return carry

        lax.fori_loop(0, tc, body, 0)

    def drain(buf):
        def body(r, carry):
            for k in range(TOP_K):
                row_copy(buf, k, 0, 0).wait()
            return carry

        lax.fori_loop(0, tc, body, 0)

    cur = i & 1

    @pl.when(i == 0)
    def _():
        issue(0, 0)

    @pl.when(i + 1 < nsteps)
    def _():
        issue(i + 1, 1 - cur)

    drain(cur)
    gates = gate_ref[...]
    acc = x_ref[...]
    for k in range(TOP_K):
        acc = acc + gates[:, k:k + 1] * buf_sc[cur, k]

    @pl.when(i < na)
    def _():
        oa_ref[...] = acc

    @pl.when(i >= na)
    def _():
        ob_ref[...] = acc


def _combine(slot_flat, x1, gates, yb, *, n_first, tc):
    n, d = x1.shape
    assert n_first % tc == 0 and (n - n_first) % tc == 0
    na = n_first // tc
    amap, bmap = _two_group_maps(na)
    kern = functools.partial(_combine_kernel, tc=tc, n=n, na=na)
    return pl.pallas_call(
        kern,
        grid_spec=pltpu.PrefetchScalarGridSpec(
            num_scalar_prefetch=1,
            grid=(n // tc,),
            in_specs=[
                pl.BlockSpec((tc, d), lambda i, *_: (i, 0)),
                pl.BlockSpec((tc, TOP_K), lambda i, *_: (i, 0)),
                pl.BlockSpec(memory_space=pl.ANY),
            ],
            out_specs=[
                pl.BlockSpec((tc, d), lambda i, *_: (amap(i), 0)),
                pl.BlockSpec((tc, d), lambda i, *_: (bmap(i), 0)),
            ],
            scratch_shapes=[pltpu.VMEM((2, TOP_K, tc, d), F32), pltpu.SemaphoreType.DMA((2,))],
        ),
        out_shape=[
            jax.ShapeDtypeStruct((n_first, d), F32),
            jax.ShapeDtypeStruct((n - n_first, d), F32),
        ],
        compiler_params=_params(("arbitrary",), 48),
        name="combine",
    )(slot_flat, x1, gates, yb)


def _rotate_half_cols(w):
    half = w.shape[-1] // 2
    return jnp.concatenate([-w[..., half:], w[..., :half]], axis=-1)


def _rope_tables(pos):
    inv = 1.0 / (ROPE_THETA ** (jnp.arange(0, QK_ROPE, 2, dtype=F32) / QK_ROPE))
    ang = pos.astype(F32)[:, None] * inv[None, :]
    cos, sin = jnp.cos(ang), jnp.sin(ang)
    return jnp.concatenate([cos, cos], axis=-1), jnp.concatenate([sin, sin], axis=-1)


def kernel(x_prompt, x_sample, cache_ckv, cache_kpe, state_pool, attn_norm, w_in, q_a_norm, w_q_b, kv_a_norm, w_kv_b, q_norm, k_norm, w_pool, pool_scale, w_out, ffn_norm, w_router, b_router, w_gate, b_gate, w_up, b_up, w_down, b_down):
    bp, seq, d = x_prompt.shape
    nb, t_dec, _ = x_sample.shape
    past = cache_ckv.shape[1]
    kv_lora = cache_ckv.shape[2]
    d_pool = state_pool.shape[2]
    q_lora = q_a_norm.shape[0]
    nh = w_q_b.shape[1] // QK_HEAD
    ne = w_router.shape[1]
    assert bp == 1 and seq >= POOL_STATE and t_dec >= POOL_STATE
    assert w_in.shape[1] == d_pool + q_lora + kv_lora + QK_ROPE

    n_p, n_s = bp * seq, nb * t_dec
    n = n_p + n_s
    tm = _tile(math.gcd(n_p, n_s), 512)
    xa, xb = x_prompt.reshape(n_p, d), x_sample.reshape(n_s, d)

    cos_p, sin_p = _rope_tables(jnp.arange(seq))
    cos_s, sin_s = _rope_tables(past + jnp.arange(t_dec))
    cos2 = jnp.concatenate([cos_p] * bp + [cos_s] * nb, axis=0)
    sin2 = jnp.concatenate([sin_p] * bp + [sin_s] * nb, axis=0)

    tn_in = _tile(kv_lora, 512, 128)
    while d_pool % tn_in or q_lora % tn_in:
        tn_in //= 2
    assert tn_in >= 2 * QK_ROPE
    w_kpe = w_in[:, d_pool + q_lora + kv_lora:]
    w_all = jnp.concatenate(
        [w_in[:, :d_pool + q_lora + kv_lora], w_kpe, _rotate_half_cols(w_kpe),
         jnp.zeros((d, tn_in - 2 * QK_ROPE), w_in.dtype)], axis=1).astype(BF16)
    wq = w_q_b.reshape(q_lora, nh, QK_HEAD)
    wq = jnp.concatenate([wq, _rotate_half_cols(wq[..., QK_NOPE:])], axis=-1)
    wq = wq.transpose(1, 0, 2).astype(BF16)
    wkv = w_kv_b.reshape(kv_lora, nh, QK_NOPE + V_HEAD).transpose(1, 0, 2).astype(BF16)
    w_out_b = w_out.astype(BF16)

    u, cqn, ckv, kpe = _inproj(xa, xb, attn_norm.reshape(1, d), w_all, q_a_norm.reshape(1, q_lora),
                               kv_a_norm.reshape(1, kv_lora), cos2, sin2,
                               d_pool=d_pool, q_lora=q_lora, kv_lora=kv_lora, tn=tn_in, tm=tm)
    hb = 4 if nh % 4 == 0 else (2 if nh % 2 == 0 else 1)
    q = _qproj(cqn, wq, q_norm.reshape(1, QK_HEAD), cos2, sin2, tm=tm, hb=hb)
    t_att = _tile(math.gcd(seq, n), 512, CHUNK)
    k_new, v_new, vt_new = _kvexp(ckv, kpe, wkv, k_norm.reshape(1, QK_HEAD), tm=t_att, hb=hb, with_vt=True)
    n_c = nb * past
    k_past, v_past = _kvexp(cache_ckv.reshape(n_c, kv_lora), cache_kpe.reshape(n_c, QK_ROPE), wkv,
                            k_norm.reshape(1, QK_HEAD), tm=_tile(n_c, 512), hb=hb, with_vt=False)

    attn_p = _attn_prompt(q, k_new, vt_new, seq=seq, tk=t_att)
    attn_s = _attn_sample(q, k_new, v_new, k_past, v_past, nb=nb, t=t_dec, past=past, row0=n_p, hb=hb)

    u_p = u[:n_p].reshape(bp, seq, d_pool)
    u_s = u[n_p:].reshape(nb, t_dec, d_pool)
    w_pool_b = w_pool.astype(BF16)
    scale2 = pool_scale.reshape(1, d_pool)
    halo_p = jnp.zeros((bp, POOL_HALO, d_pool), F32)
    halo_s = jnp.concatenate([jnp.zeros((nb, 1, d_pool), F32), state_pool], axis=1)
    pool_p = _pool(u_p, halo_p, w_pool_b, scale2, pos0=0, tm=_tile(seq, 512))
    pool_s = _pool(u_s, halo_s, w_pool_b, scale2, pos0=past, tm=t_dec)

    x1 = _outproj(pool_p, pool_s, attn_p, attn_s, w_out_b[:d_pool], w_out_b[d_pool:], xa, xb,
                  tm=tm, tn=_tile(d, 1024, 128))

    hp, top_idx, gates, rank, counts = _router(x1, ffn_norm.reshape(1, d), w_router.T.astype(BF16),
                                               b_router.reshape(ne, 1), tm=tm)
    tm_moe = 512
    counts = counts.reshape(ne)
    padded = (counts + tm_moe - 1) // tm_moe * tm_moe
    pend = jnp.cumsum(padded)
    pstart = pend - padded
    experts = jnp.arange(ne, dtype=jnp.int32)
    first_slot = jnp.sum(jnp.where(top_idx[..., None] == experts, pstart.astype(jnp.int32), 0), axis=-1)
    slot_flat = (first_slot + rank).reshape(TOP_K * n).astype(jnp.int32)
    n_blocks = -(-(n * TOP_K) // tm_moe) + ne
    block_row0 = jnp.arange(n_blocks, dtype=jnp.int32) * tm_moe
    block_expert = jnp.minimum(jnp.sum(block_row0[:, None] >= pend[None, :], axis=1), ne - 1).astype(jnp.int32)
    n_used = (pend[-1:] // tm_moe).astype(jnp.int32)
    real_end = jnp.sum(jnp.where(block_expert[:, None] == experts, (pstart + counts).astype(jnp.int32), 0), axis=1)
    n_valid = jnp.where(block_row0 < pend[-1], jnp.clip(real_end - block_row0, 0, tm_moe), 0).astype(jnp.int32)

    xb = _dispatch(slot_flat, (pstart + counts).astype(jnp.int32), (padded - counts).astype(jnp.int32), n_used, hp,
                   tm=tm_moe, nblk=n_blocks, tt=_tile(n, 256, 8))
    dff = w_gate.shape[2]
    h1 = _gateup(block_expert, n_used, n_valid, xb, w_gate, w_up, b_gate, b_up,
                 tm=tm_moe, tn=_tile(dff, 512, 128), nsub=2)
    yb = _down(block_expert, n_used, n_valid, h1, w_down, b_down, tm=tm_moe, tn=_tile(d, 1024, 128), nsub=2)
    y_p, y_s = _combine(slot_flat, x1, gates.T, yb, n_first=n_p, tc=_tile(math.gcd(n_p, n_s), 128, 8))

    return (
        y_p.reshape(bp, seq, d),
        y_s.reshape(nb, t_dec, d),
        ckv[:n_p].reshape(bp, seq, kv_lora),
        kpe[:n_p].reshape(bp, seq, QK_ROPE),
        u_p[:, seq - POOL_STATE:],
        ckv[n_p:].reshape(nb, t_dec, kv_lora),
        kpe[n_p:].reshape(nb, t_dec, QK_ROPE),
        u_s[:, t_dec - POOL_STATE:],
    )
```

```python
import functools
import math

import jax
import jax.numpy as jnp
from jax import lax
from jax.experimental import pallas as pl
from jax.experimental.pallas import tpu as pltpu

CHUNK = 64
POOL_WINDOWS = (2, 4, 8, 16)
POOL_STATE = max(POOL_WINDOWS) - 1
POOL_HALO = POOL_STATE + 1
QK_NOPE = 128
QK_ROPE = 64
QK_HEAD = QK_NOPE + QK_ROPE
V_HEAD = 128
ROPE_THETA = 10000.0
TOP_K = 4
SWIGLU_LIMIT = 7.0
SWIGLU_ALPHA = 1.702
EPS = 1e-6
LOG2_E = 1.4426950408889634

F32 = jnp.float32
BF16 = jnp.bfloat16
MIB = 1024 * 1024
NT_DIMS = (((1,), (1,)), ((), ()))


def _params(semantics, vmem_mib=None):
    kw = dict(dimension_semantics=semantics)
    if vmem_mib is not None:
        kw["vmem_limit_bytes"] = vmem_mib * MIB
    return pltpu.CompilerParams(**kw)


def _tile(n, pref, mult=16):
    t = min(pref, n)
    t -= t % mult
    while t > mult and n % t:
        t -= mult
    assert t >= mult and n % t == 0, (n, pref, mult)
    return t


def _inproj_kernel(xa_ref, xb_ref, g_ref, w_ref, qg_ref, kvg_ref, cos_ref, sin_ref,
                   u_ref, cqn_ref, ckv_ref, kpe_ref, h_sc, seg_sc, *, nu, nq, nkv, tn, na):
    i = pl.program_id(0)
    j = pl.program_id(1)

    def normalise(x_ref):
        x = x_ref[...]
        ms = jnp.mean(x * x, axis=-1, keepdims=True)
        h_sc[...] = (x * lax.rsqrt(ms + EPS) * g_ref[...]).astype(BF16)

    @pl.when(jnp.logical_and(j == 0, i < na))
    def _():
        normalise(xa_ref)

    @pl.when(jnp.logical_and(j == 0, i >= na))
    def _():
        normalise(xb_ref)

    y = jnp.dot(h_sc[...], w_ref[...], preferred_element_type=F32)

    @pl.when(j < nu)
    def _():
        u_ref[...] = y

    def finish(n, gain_ref, out_ref):
        parts = [seg_sc[t] for t in range(n - 1)] + [y]
        ss = jnp.sum(parts[0] * parts[0], axis=-1, keepdims=True)
        for p in parts[1:]:
            ss = ss + jnp.sum(p * p, axis=-1, keepdims=True)
        r = lax.rsqrt(ss / (n * tn) + EPS)
        for t, p in enumerate(parts):
            out_ref[:, t * tn:(t + 1) * tn] = (p * r * gain_ref[:, t * tn:(t + 1) * tn]).astype(out_ref.dtype)

    def segment(first, n, gain_ref, out_ref):
        for t in range(n - 1):
            @pl.when(j == first + t)
            def _(t=t):
                seg_sc[t] = y

        @pl.when(j == first + n - 1)
        def _():
            finish(n, gain_ref, out_ref)

    segment(nu, nq, qg_ref, cqn_ref)
    segment(nu + nq, nkv, kvg_ref, ckv_ref)

    @pl.when(j == nu + nq + nkv)
    def _():
        kpe_ref[...] = y[:, :QK_ROPE] * cos_ref[...] + y[:, QK_ROPE:2 * QK_ROPE] * sin_ref[...]


def _two_group_maps(na):
    return (lambda i: jnp.minimum(i, na - 1)), (lambda i: jnp.maximum(i - na, 0))


def _inproj(xa, xb, attn_norm, w_all, q_a_norm, kv_a_norm, cos2, sin2, *, d_pool, q_lora, kv_lora, tn, tm):
    d = xa.shape[1]
    assert xa.shape[0] % tm == 0 and xb.shape[0] % tm == 0
    na = xa.shape[0] // tm
    n = xa.shape[0] + xb.shape[0]
    amap, bmap = _two_group_maps(na)
    nu, nq, nkv = d_pool // tn, q_lora // tn, kv_lora // tn
    nj = nu + nq + nkv + 1
    assert w_all.shape == (d, nj * tn)
    kern = functools.partial(_inproj_kernel, nu=nu, nq=nq, nkv=nkv, tn=tn, na=na)
    return pl.pallas_call(
        kern,
        grid=(n // tm, nj),
        in_specs=[
            pl.BlockSpec((tm, d), lambda i, j: (amap(i), 0)),
            pl.BlockSpec((tm, d), lambda i, j: (bmap(i), 0), pipeline_mode=pl.Buffered(1)),
            pl.BlockSpec((1, d), lambda i, j: (0, 0)),
            pl.BlockSpec((d, tn), lambda i, j: (0, j)),
            pl.BlockSpec((1, q_lora), lambda i, j: (0, 0)),
            pl.BlockSpec((1, kv_lora), lambda i, j: (0, 0)),
            pl.BlockSpec((tm, QK_ROPE), lambda i, j: (i, 0)),
            pl.BlockSpec((tm, QK_ROPE), lambda i, j: (i, 0)),
        ],
        out_specs=[
            pl.BlockSpec((tm, tn), lambda i, j: (i, jnp.minimum(j, nu - 1))),
            pl.BlockSpec((tm, q_lora), lambda i, j: (i, 0)),
            pl.BlockSpec((tm, kv_lora), lambda i, j: (i, 0)),
            pl.BlockSpec((tm, QK_ROPE), lambda i, j: (i, 0)),
        ],
        out_shape=[
            jax.ShapeDtypeStruct((n, d_pool), F32),
            jax.ShapeDtypeStruct((n, q_lora), BF16),
            jax.ShapeDtypeStruct((n, kv_lora), F32),
            jax.ShapeDtypeStruct((n, QK_ROPE), F32),
        ],
        scratch_shapes=[
            pltpu.VMEM((tm, d), BF16),
            pltpu.VMEM((max(nq, nkv, 2) - 1, tm, tn), F32),
        ],
        compiler_params=_params(("parallel", "arbitrary"), 58),
        name="inproj",
    )(xa, xb, attn_norm, w_all, q_a_norm, kv_a_norm, cos2, sin2)


def _qproj_kernel(c_ref, w_ref, g_ref, cos_ref, sin_ref, q_ref, *, hb, scale):
    c = c_ref[...]
    cos = cos_ref[...]
    sin = sin_ref[...]
    g = g_ref[...]
    for h in range(hb):
        y = jnp.dot(c, w_ref[h], preferred_element_type=F32)
        nope = y[:, :QK_NOPE]
        rope = y[:, QK_NOPE:QK_HEAD] * cos + y[:, QK_HEAD:QK_HEAD + QK_ROPE] * sin
        ss = jnp.sum(nope * nope, axis=-1, keepdims=True) + jnp.sum(rope * rope, axis=-1, keepdims=True)
        r = lax.rsqrt(ss / QK_HEAD + EPS) * scale
        q_ref[h, :, :QK_NOPE] = (nope * r * g[:, :QK_NOPE]).astype(BF16)
        q_ref[h, :, QK_NOPE:] = (rope * r * g[:, QK_NOPE:]).astype(BF16)


def _qproj(cqn, wq, q_norm, cos2, sin2, *, tm, hb):
    n, q_lora = cqn.shape
    nh = wq.shape[0]
    kern = functools.partial(_qproj_kernel, hb=hb, scale=QK_HEAD ** -0.5 * LOG2_E)
    return pl.pallas_call(
        kern,
        grid=(n // tm, nh // hb),
        in_specs=[
            pl.BlockSpec((tm, q_lora), lambda i, j: (i, 0)),
            pl.BlockSpec((hb, q_lora, QK_HEAD + QK_ROPE), lambda i, j: (j, 0, 0)),
            pl.BlockSpec((1, QK_HEAD), lambda i, j: (0, 0)),
            pl.BlockSpec((tm, QK_ROPE), lambda i, j: (i, 0)),
            pl.BlockSpec((tm, QK_ROPE), lambda i, j: (i, 0)),
        ],
        out_specs=pl.BlockSpec((hb, tm, QK_HEAD), lambda i, j: (j, i, 0)),
        out_shape=jax.ShapeDtypeStruct((nh, n, QK_HEAD), BF16),
        compiler_params=_params(("parallel", "arbitrary"), 48),
        name="qproj",
    )(cqn, wq, q_norm, cos2, sin2)


def _kvexp_kernel(c_ref, pe_ref, w_ref, g_ref, k_ref, v_ref, *maybe_vt_ref, hb):
    c = c_ref[...].astype(BF16)
    pe = pe_ref[...]
    g = g_ref[...]
    pe_ss = jnp.sum(pe * pe, axis=-1, keepdims=True)
    for h in range(hb):
        y = jnp.dot(c, w_ref[h], preferred_element_type=F32)
        nope = y[:, :QK_NOPE]
        r = lax.rsqrt((jnp.sum(nope * nope, axis=-1, keepdims=True) + pe_ss) / QK_HEAD + EPS)
        k_ref[h, :, :QK_NOPE] = (nope * r * g[:, :QK_NOPE]).astype(BF16)
        k_ref[h, :, QK_NOPE:] = (pe * r * g[:, QK_NOPE:]).astype(BF16)
        v = y[:, QK_NOPE:]
        v_ref[h] = v.astype(BF16)
        for vt_ref in maybe_vt_ref:
            vt_ref[h, 0] = v.T.astype(BF16)


def _kvexp(ckv, kpe, wkv, k_norm, *, tm, hb, with_vt):
    n, kv_lora = ckv.shape
    nh = wkv.shape[0]
    kern = functools.partial(_kvexp_kernel, hb=hb)
    vt_spec = [pl.BlockSpec((hb, 1, V_HEAD, tm), lambda i, j: (j, i, 0, 0))] if with_vt else []
    vt_shape = [jax.ShapeDtypeStruct((nh, n // tm, V_HEAD, tm), BF16)] if with_vt else []
    return pl.pallas_call(
        kern,
        grid=(n // tm, nh // hb),
        in_specs=[
            pl.BlockSpec((tm, kv_lora), lambda i, j: (i, 0)),
            pl.BlockSpec((tm, QK_ROPE), lambda i, j: (i, 0)),
            pl.BlockSpec((hb, kv_lora, QK_NOPE + V_HEAD), lambda i, j: (j, 0, 0)),
            pl.BlockSpec((1, QK_HEAD), lambda i, j: (0, 0)),
        ],
        out_specs=[
            pl.BlockSpec((hb, tm, QK_HEAD), lambda i, j: (j, i, 0)),
            pl.BlockSpec((hb, tm, V_HEAD), lambda i, j: (j, i, 0)),
        ] + vt_spec,
        out_shape=[
            jax.ShapeDtypeStruct((nh, n, QK_HEAD), BF16),
            jax.ShapeDtypeStruct((nh, n, V_HEAD), BF16),
        ] + vt_shape,
        compiler_params=_params(("parallel", "arbitrary"), 48),
        name="kvexp",
    )(ckv, kpe, wkv, k_norm)


def _attn_prompt_kernel(q_ref, k_ref, vt_ref, o_ref, st_sc, m_sc, l_sc, acc_sc, *, tk):
    i = pl.program_id(1)
    m_sc[...] = jnp.full(m_sc.shape, -jnp.inf, F32)
    l_sc[...] = jnp.zeros(l_sc.shape, F32)
    acc_sc[...] = jnp.zeros(acc_sc.shape, F32)

    def qk(c, slot, q0=0):
        off = pl.multiple_of(c * tk, tk)
        st_sc[slot, :, q0:] = lax.dot_general(k_ref[0, pl.ds(off, tk), :], q_ref[0, q0:, :], NT_DIMS,
                                              preferred_element_type=F32)

    def update(c, slot, q0=0, diagonal=False):
        st = st_sc[slot, :, q0:]
        if diagonal:
            key_chunk = lax.broadcasted_iota(jnp.int32, st.shape, 0) // CHUNK
            qry_chunk = lax.broadcasted_iota(jnp.int32, st.shape, 1) // CHUNK
            st = jnp.where(key_chunk <= qry_chunk, st, -jnp.inf)
        m_old = m_sc[:, q0:]
        m_new = jnp.maximum(m_old, jnp.max(st, axis=0, keepdims=True))
        alpha = jnp.exp2(m_old - m_new)
        p = jnp.exp2(st - m_new)
        l_sc[:, q0:] = alpha * l_sc[:, q0:] + jnp.sum(p, axis=0, keepdims=True)
        acc_sc[:, q0:] = alpha * acc_sc[:, q0:] + jnp.dot(vt_ref[0, c], p.astype(BF16),
                                                          preferred_element_type=F32)
        m_sc[:, q0:] = m_new

    qk(0, 0)

    def pair(j, carry):
        c = 2 * j
        qk(c + 1, 1)
        update(c, 0)
        qk(c + 2, 0)
        update(c + 1, 1)
        return carry

    lax.fori_loop(0, i, pair, 0)
    qk(2 * i + 1, 1, q0=tk)
    update(2 * i, 0, diagonal=True)
    update(2 * i + 1, 1, q0=tk, diagonal=True)

    o_ref[...] = (acc_sc[...] / l_sc[...]).T.astype(o_ref.dtype)


def _attn_prompt(q, k, vt, *, seq, tk):
    nh = q.shape[0]
    tq = 2 * tk
    assert seq % tq == 0 and tk % CHUNK == 0 and vt.shape[3] == tk
    kern = functools.partial(_attn_prompt_kernel, tk=tk)
    return pl.pallas_call(
        kern,
        grid=(nh, seq // tq),
        in_specs=[
            pl.BlockSpec((1, tq, QK_HEAD), lambda h, i: (h, i, 0)),
            pl.BlockSpec((1, seq, QK_HEAD), lambda h, i: (h, 0, 0)),
            pl.BlockSpec((1, seq // tk, V_HEAD, tk), lambda h, i: (h, 0, 0, 0)),
        ],
        out_specs=pl.BlockSpec((tq, V_HEAD), lambda h, i: (i, h)),
        out_shape=jax.ShapeDtypeStruct((seq, nh * V_HEAD), BF16),
        scratch_shapes=[
            pltpu.VMEM((2, tk, tq), F32),
            pltpu.VMEM((1, tq), F32),
            pltpu.VMEM((1, tq), F32),
            pltpu.VMEM((V_HEAD, tq), F32),
        ],
        compiler_params=_params(("parallel", "arbitrary"), 48),
        name="attn_prompt",
    )(q, k, vt)


def _attn_sample_kernel(q_ref, kp_ref, vp_ref, kn_ref, vn_ref, o_ref, *, hb):
    for h in range(hb):
        q = q_ref[h]
        s1 = lax.dot_general(q, kp_ref[h], NT_DIMS, preferred_element_type=F32)
        s2 = lax.dot_general(q, kn_ref[h], NT_DIMS, preferred_element_type=F32)
        m = jnp.maximum(jnp.max(s1, axis=-1, keepdims=True), jnp.max(s2, axis=-1, keepdims=True))
        p1 = jnp.exp2(s1 - m)
        p2 = jnp.exp2(s2 - m)
        l = jnp.sum(p1, axis=-1, keepdims=True) + jnp.sum(p2, axis=-1, keepdims=True)
        o = (jnp.dot(p1.astype(BF16), vp_ref[h], preferred_element_type=F32)
             + jnp.dot(p2.astype(BF16), vn_ref[h], preferred_element_type=F32))
        o_ref[:, h * V_HEAD:(h + 1) * V_HEAD] = (o / l).astype(o_ref.dtype)


def _attn_sample(q, k_new, v_new, k_past, v_past, *, nb, t, past, row0, hb):
    nh = q.shape[0]
    assert row0 % t == 0 and nh % hb == 0
    r0 = row0 // t
    return pl.pallas_call(
        functools.partial(_attn_sample_kernel, hb=hb),
        grid=(nh // hb, nb),
        in_specs=[
            pl.BlockSpec((hb, t, QK_HEAD), lambda h, b: (h, r0 + b, 0)),
            pl.BlockSpec((hb, past, QK_HEAD), lambda h, b: (h, b, 0)),
            pl.BlockSpec((hb, past, V_HEAD), lambda h, b: (h, b, 0)),
            pl.BlockSpec((hb, t, QK_HEAD), lambda h, b: (h, r0 + b, 0)),
            pl.BlockSpec((hb, t, V_HEAD), lambda h, b: (h, r0 + b, 0)),
        ],
        out_specs=pl.BlockSpec((t, hb * V_HEAD), lambda h, b: (b, h)),
        out_shape=jax.ShapeDtypeStruct((nb * t, nh * V_HEAD), BF16),
        compiler_params=_params(("parallel", "arbitrary"), 48),
        name="attn_sample",
    )(q, k_past, v_past, k_new, v_new)


def _pool_kernel(u_ref, prev_ref, w_ref, sc_ref, o_ref, ext_sc, *, tm, pg, pos0):
    t = pl.program_id(1)

    @pl.when(t == 0)
    def _():
        ext_sc[0:POOL_HALO, :] = prev_ref[0]

    @pl.when(t > 0)
    def _():
        ext_sc[0:POOL_HALO, :] = ext_sc[tm:tm + POOL_HALO, :]

    u = u_ref[0]
    ext_sc[POOL_HALO:, :] = u
    pos = pos0 + t * tm + lax.broadcasted_iota(jnp.int32, (tm, 1), 0)
    for g, w in enumerate(POOL_WINDOWS):
        cols = slice(g * pg, (g + 1) * pg)
        win = ext_sc[POOL_HALO:POOL_HALO + tm, cols]
        for back in range(1, w):
            win = win + ext_sc[POOL_HALO - back:POOL_HALO - back + tm, cols]
        inv_cnt = 1.0 / jnp.minimum(pos + 1, w).astype(F32)
        pooled = win * inv_cnt - u[:, cols]
        y = jnp.dot(pooled.astype(BF16), w_ref[g], preferred_element_type=F32)
        o_ref[:, cols] = (y * sc_ref[:, cols]).astype(o_ref.dtype)


def _pool(u, prev_halo, w_pool, pool_scale, *, pos0, tm):
    b, t, c = u.shape
    pg = c // len(POOL_WINDOWS)
    nt = t // tm
    kern = functools.partial(_pool_kernel, tm=tm, pg=pg, pos0=pos0)
    return pl.pallas_call(
        kern,
        grid=(b, nt),
        in_specs=[
            pl.BlockSpec((1, tm, c), lambda bi, ti: (bi, ti, 0)),
            pl.BlockSpec((1, POOL_HALO, c), lambda bi, ti: (bi, 0, 0)),
            pl.BlockSpec((len(POOL_WINDOWS), pg, pg), lambda bi, ti: (0, 0, 0)),
            pl.BlockSpec((1, c), lambda bi, ti: (0, 0)),
        ],
        out_specs=pl.BlockSpec((tm, c), lambda bi, ti: (bi * nt + ti, 0)),
        out_shape=jax.ShapeDtypeStruct((b * t, c), BF16),
        scratch_shapes=[pltpu.VMEM((tm + POOL_HALO, c), F32)],
        compiler_params=_params(("arbitrary", "arbitrary"), 48),
        name="pool",
    )(u, prev_halo, w_pool, pool_scale)


def _outproj_kernel(pa_ref, pb_ref, aa_ref, ab_ref, wp_ref, wa_ref, xa_ref, xb_ref, o_ref, *, na):
    i = pl.program_id(1)

    def run(p_ref, a_ref, x_ref):
        acc = jnp.dot(p_ref[...], wp_ref[...], preferred_element_type=F32)
        acc = acc + jnp.dot(a_ref[...], wa_ref[...], preferred_element_type=F32)
        o_ref[...] = x_ref[...] + acc

    @pl.when(i < na)
    def _():
        run(pa_ref, aa_ref, xa_ref)

    @pl.when(i >= na)
    def _():
        run(pb_ref, ab_ref, xb_ref)


def _outproj(pool_a, pool_b, attn_a, attn_b, w_p, w_a, xa, xb, *, tm, tn):
    d = xa.shape[1]
    assert xa.shape[0] % tm == 0 and xb.shape[0] % tm == 0
    na = xa.shape[0] // tm
    n = xa.shape[0] + xb.shape[0]
    amap, bmap = _two_group_maps(na)
    dp, da = pool_a.shape[1], attn_a.shape[1]
    return pl.pallas_call(
        functools.partial(_outproj_kernel, na=na),
        grid=(d // tn, n // tm),
        in_specs=[
            pl.BlockSpec((tm, dp), lambda j, i: (amap(i), 0)),
            pl.BlockSpec((tm, dp), lambda j, i: (bmap(i), 0)),
            pl.BlockSpec((tm, da), lambda j, i: (amap(i), 0)),
            pl.BlockSpec((tm, da), lambda j, i: (bmap(i), 0)),
            pl.BlockSpec((dp, tn), lambda j, i: (0, j)),
            pl.BlockSpec((da, tn), lambda j, i: (0, j)),
            pl.BlockSpec((tm, tn), lambda j, i: (amap(i), j)),
            pl.BlockSpec((tm, tn), lambda j, i: (bmap(i), j)),
        ],
        out_specs=pl.BlockSpec((tm, tn), lambda j, i: (i, j)),
        out_shape=jax.ShapeDtypeStruct((n, d), F32),
        compiler_params=_params(("parallel", "arbitrary"), 48),
        name="outproj",
    )(pool_a, pool_b, attn_a, attn_b, w_p, w_a, xa, xb)


def _router_kernel(x_ref, g_ref, wr_ref, br_ref, hp_ref, idx_ref, gate_ref, rank_ref, cnt_ref, run_sc, *, ne):
    i = pl.program_id(0)

    @pl.when(i == 0)
    def _():
        run_sc[...] = jnp.zeros(run_sc.shape, F32)

    x = x_ref[...]
    tm, d = x.shape
    ms = jnp.mean(x * x, axis=-1, keepdims=True)
    h = x * lax.rsqrt(ms + EPS) * g_ref[...]
    hb = h.astype(BF16)

    bits = lax.bitcast_convert_type(hb.astype(F32), jnp.uint32)
    hp_ref[...] = (bits[:, :d // 2] >> 16) | (bits[:, d // 2:] & jnp.uint32(0xFFFF0000))

    logits = lax.dot_general(wr_ref[...], hb, NT_DIMS, preferred_element_type=F32) + br_ref[...]
    eidx = lax.broadcasted_iota(jnp.int32, logits.shape, 0)
    work = logits
    vals, idxs = [], []
    for _ in range(TOP_K):
        m = jnp.max(work, axis=0, keepdims=True)
        sel = jnp.min(jnp.where(work == m, eidx, ne), axis=0, keepdims=True)
        vals.append(m)
        idxs.append(sel)
        work = jnp.where(eidx == sel, -jnp.inf, work)
    exps = [jnp.exp(v - vals[0]) for v in vals]
    den = exps[0]
    for e in exps[1:]:
        den = den + e

    hits = [(eidx == sel).astype(F32) for sel in idxs]
    onehot = hits[0]
    for hit in hits[1:]:
        onehot = onehot + hit
    before = (lax.broadcasted_iota(jnp.int32, (tm, tm), 0) < lax.broadcasted_iota(jnp.int32, (tm, tm), 1))
    prefix = jnp.dot(onehot.astype(BF16), before.astype(BF16), preferred_element_type=F32)
    base = prefix + run_sc[...]
    for k in range(TOP_K):
        idx_ref[k:k + 1, :] = idxs[k]
        gate_ref[k:k + 1, :] = exps[k] / den
        rank_ref[k:k + 1, :] = jnp.sum(hits[k] * base, axis=0, keepdims=True).astype(jnp.int32)
    run_sc[...] = run_sc[...] + jnp.sum(onehot, axis=1, keepdims=True)
    cnt_ref[...] = run_sc[...].astype(jnp.int32)


def _router(x1, ffn_norm, wr_t, b_router, *, tm):
    n, d = x1.shape
    ne = wr_t.shape[0]
    kern = functools.partial(_router_kernel, ne=ne)
    return pl.pallas_call(
        kern,
        grid=(n // tm,),
        in_specs=[
            pl.BlockSpec((tm, d), lambda i: (i, 0)),
            pl.BlockSpec((1, d), lambda i: (0, 0)),
            pl.BlockSpec((ne, d), lambda i: (0, 0)),
            pl.BlockSpec((ne, 1), lambda i: (0, 0)),
        ],
        out_specs=[
            pl.BlockSpec((tm, d // 2), lambda i: (i, 0)),
            pl.BlockSpec((TOP_K, tm), lambda i: (0, i)),
            pl.BlockSpec((TOP_K, tm), lambda i: (0, i)),
            pl.BlockSpec((TOP_K, tm), lambda i: (0, i)),
            pl.BlockSpec((ne, 1), lambda i: (0, 0)),
        ],
        out_shape=[
            jax.ShapeDtypeStruct((n, d // 2), jnp.uint32),
            jax.ShapeDtypeStruct((TOP_K, n), jnp.int32),
            jax.ShapeDtypeStruct((TOP_K, n), F32),
            jax.ShapeDtypeStruct((TOP_K, n), jnp.int32),
            jax.ShapeDtypeStruct((ne, 1), jnp.int32),
        ],
        scratch_shapes=[pltpu.VMEM((ne, 1), F32)],
        compiler_params=_params(("arbitrary",), 48),
        name="router",
    )(x1, ffn_norm, wr_t, b_router)


def _dispatch_kernel(slot_ref, pad0_ref, padn_ref, nused_ref, h_ref, xb_ref, zero_sc, sem, *, tt, n, ne, tm, nblk):
    i = pl.program_id(0)
    zr = zero_sc.shape[0]

    def row_copy(src, r, s):
        return pltpu.make_async_copy(src.at[pl.ds(r, 1)], xb_ref.at[pl.ds(s, 1)], sem)

    def issue(r, carry):
        for k in range(TOP_K):
            row_copy(h_ref, r, slot_ref[k * n + i * tt + r]).start()
        return carry

    def drain(r, carry):
        for k in range(TOP_K):
            row_copy(h_ref, 0, 0).wait()
        return carry

    lax.fori_loop(0, tt, issue, 0)
    lax.fori_loop(0, tt, drain, 0)

    @pl.when(i == pl.num_programs(0) - 1)
    def _():
        zero_sc[...] = jnp.zeros(zero_sc.shape, zero_sc.dtype)

        def per_expert(e, carry):
            start = pad0_ref[e]
            cnt = padn_ref[e]
            lax.fori_loop(0, cnt, lambda r, c: (row_copy(zero_sc, 0, start + r).start(), c)[1], 0)
            lax.fori_loop(0, cnt, lambda r, c: (row_copy(zero_sc, 0, 0).wait(), c)[1], 0)
            return carry

        lax.fori_loop(0, ne, per_expert, 0)

        def strip_copy(blk, c):
            return pltpu.make_async_copy(zero_sc, xb_ref.at[pl.ds(blk * tm + c * zr, zr)], sem)

        def per_block(blk, carry):
            for c in range(tm // zr):
                strip_copy(blk, c).start()
            for c in range(tm // zr):
                strip_copy(blk, c).wait()
            return carry

        lax.fori_loop(nused_ref[0], nblk, per_block, 0)


def _dispatch(slot_flat, pad_start, pad_count, n_used, hp, *, tm, nblk, tt):
    n, half = hp.shape
    ne = pad_start.shape[0]
    zr = _tile(tm, 64, 8)
    kern = functools.partial(_dispatch_kernel, tt=tt, n=n, ne=ne, tm=tm, nblk=nblk)
    return pl.pallas_call(
        kern,
        grid_spec=pltpu.PrefetchScalarGridSpec(
            num_scalar_prefetch=4,
            grid=(n // tt,),
            in_specs=[pl.BlockSpec((tt, half), lambda i, *_: (i, 0))],
            out_specs=pl.BlockSpec(memory_space=pl.ANY),
            scratch_shapes=[pltpu.VMEM((zr, half), jnp.uint32), pltpu.SemaphoreType.DMA(())],
        ),
        out_shape=jax.ShapeDtypeStruct((nblk * tm, half), jnp.uint32),
        compiler_params=_params(("arbitrary",), 48),
        name="dispatch",
    )(slot_flat, pad_start, pad_count, n_used, hp)


def _unpack_rows(p):
    lo = lax.bitcast_convert_type(p << 16, F32).astype(BF16)
    hi = lax.bitcast_convert_type(p & jnp.uint32(0xFFFF0000), F32).astype(BF16)
    return lo, hi


def _gateup_kernel(be_ref, nused_ref, x_ref, wg_ref, wu_ref, bg_ref, bu_ref, h1_ref):
    i = pl.program_id(1)

    @pl.when(i < nused_ref[0])
    def _():
        lo, hi = _unpack_rows(x_ref[...])
        half = lo.shape[1]

        def proj(w_ref, b_ref):
            return (jnp.dot(lo, w_ref[0, :half, :].astype(BF16), preferred_element_type=F32)
                    + jnp.dot(hi, w_ref[0, half:, :].astype(BF16), preferred_element_type=F32) + b_ref[0])

        g = jnp.minimum(proj(wg_ref, bg_ref), SWIGLU_LIMIT)
        up = jnp.clip(proj(wu_ref, bu_ref), -SWIGLU_LIMIT, SWIGLU_LIMIT)
        h1_ref[...] = (g * jax.nn.sigmoid(SWIGLU_ALPHA * g) * (up + 1.0)).astype(h1_ref.dtype)

    @pl.when(i >= nused_ref[0])
    def _():
        h1_ref[...] = jnp.zeros(h1_ref.shape, h1_ref.dtype)


def _gateup(block_expert, n_used, xb, w_gate, w_up, b_gate, b_up, *, tm, tn):
    rows, half = xb.shape
    ne, d, dff = w_gate.shape
    nblk = rows // tm

    def xmap(j, i, be, nu):
        return (jnp.minimum(i, nu[0] - 1), 0)

    def wmap(j, i, be, nu):
        return (be[i], 0, j)

    return pl.pallas_call(
        _gateup_kernel,
        grid_spec=pltpu.PrefetchScalarGridSpec(
            num_scalar_prefetch=2,
            grid=(dff // tn, nblk),
            in_specs=[
                pl.BlockSpec((tm, half), xmap),
                pl.BlockSpec((1, d, tn), wmap),
                pl.BlockSpec((1, d, tn), wmap),
                pl.BlockSpec((1, 1, tn), wmap),
                pl.BlockSpec((1, 1, tn), wmap),
            ],
            out_specs=pl.BlockSpec((tm, tn), lambda j, i, be, nu: (i, j)),
        ),
        out_shape=jax.ShapeDtypeStruct((rows, dff), BF16),
        compiler_params=_params(("arbitrary", "arbitrary"), 60),
        name="moe_gateup",
    )(block_expert, n_used, xb, w_gate, w_up, b_gate.reshape(ne, 1, dff), b_up.reshape(ne, 1, dff))


def _down_kernel(be_ref, nused_ref, h1_ref, wd_ref, bd_ref, y_ref):
    i = pl.program_id(1)

    @pl.when(i < nused_ref[0])
    def _():
        y_ref[...] = jnp.dot(h1_ref[...], wd_ref[0].astype(BF16), preferred_element_type=F32) + bd_ref[0]

    @pl.when(i >= nused_ref[0])
    def _():
        y_ref[...] = jnp.zeros(y_ref.shape, y_ref.dtype)


def _down(block_expert, n_used, h1, w_down, b_down, *, tm, tn):
    rows, dff = h1.shape
    ne, _, d = w_down.shape
    nblk = rows // tm

    def wmap(j, i, be, nu):
        return (be[i], 0, j)

    return pl.pallas_call(
        _down_kernel,
        grid_spec=pltpu.PrefetchScalarGridSpec(
            num_scalar_prefetch=2,
            grid=(d // tn, nblk),
            in_specs=[
                pl.BlockSpec((tm, dff), lambda j, i, be, nu: (jnp.minimum(i, nu[0] - 1), 0)),
                pl.BlockSpec((1, dff, tn), wmap),
                pl.BlockSpec((1, 1, tn), wmap),
            ],
            out_specs=pl.BlockSpec((tm, tn), lambda j, i, be, nu: (i, j)),
        ),
        out_shape=jax.ShapeDtypeStruct((rows, d), F32),
        compiler_params=_params(("arbitrary", "arbitrary"), 60),
        name="moe_down",
    )(block_expert, n_used, h1, w_down, b_down.reshape(ne, 1, d))


def _combine_kernel(slot_ref, x_ref, gate_ref, yb_ref, oa_ref, ob_ref, buf_sc, sem, *, tc, n, na):
    i = pl.program_id(0)
    nsteps = pl.num_programs(0)

    def row_copy(buf, k, r, s):
        return pltpu.make_async_copy(yb_ref.at[pl.ds(s, 1)], buf_sc.at[buf, k, pl.ds(r, 1)], sem.at[buf])

    def issue(step, buf):
        def body(r, carry):
            for k in range(TOP_K):
                row_copy(buf, k, r, slot_ref[k * n + step * tc + r]).start()
            return carry

        lax.fori_loop(0, tc, body, 0)

    def drain(buf):
        def body(r, carry):
            for k in range(TOP_K):
                row_copy(buf, k, 0, 0).wait()
            return carry

        lax.fori_loop(0, tc, body, 0)

    cur = i & 1

    @pl.when(i == 0)
    def _():
        issue(0, 0)

    @pl.when(i + 1 < nsteps)
    def _():
        issue(i + 1, 1 - cur)

    drain(cur)
    gates = gate_ref[...]
    acc = x_ref[...]
    for k in range(TOP_K):
        acc = acc + gates[:, k:k + 1] * buf_sc[cur, k]

    @pl.when(i < na)
    def _():
        oa_ref[...] = acc

    @pl.when(i >= na)
    def _():
        ob_ref[...] = acc


def _combine(slot_flat, x1, gates, yb, *, n_first, tc):
    n, d = x1.shape
    assert n_first % tc == 0 and (n - n_first) % tc == 0
    na = n_first // tc
    amap, bmap = _two_group_maps(na)
    kern = functools.partial(_combine_kernel, tc=tc, n=n, na=na)
    return pl.pallas_call(
        kern,
        grid_spec=pltpu.PrefetchScalarGridSpec(
            num_scalar_prefetch=1,
            grid=(n // tc,),
            in_specs=[
                pl.BlockSpec((tc, d), lambda i, *_: (i, 0)),
                pl.BlockSpec((tc, TOP_K), lambda i, *_: (i, 0)),
                pl.BlockSpec(memory_space=pl.ANY),
            ],
            out_specs=[
                pl.BlockSpec((tc, d), lambda i, *_: (amap(i), 0)),
                pl.BlockSpec((tc, d), lambda i, *_: (bmap(i), 0)),
            ],
            scratch_shapes=[pltpu.VMEM((2, TOP_K, tc, d), F32), pltpu.SemaphoreType.DMA((2,))],
        ),
        out_shape=[
            jax.ShapeDtypeStruct((n_first, d), F32),
            jax.ShapeDtypeStruct((n - n_first, d), F32),
        ],
        compiler_params=_params(("arbitrary",), 48),
        name="combine",
    )(slot_flat, x1, gates, yb)


def _rotate_half_cols(w):
    half = w.shape[-1] // 2
    return jnp.concatenate([-w[..., half:], w[..., :half]], axis=-1)


def _rope_tables(pos):
    inv = 1.0 / (ROPE_THETA ** (jnp.arange(0, QK_ROPE, 2, dtype=F32) / QK_ROPE))
    ang = pos.astype(F32)[:, None] * inv[None, :]
    cos, sin = jnp.cos(ang), jnp.sin(ang)
    return jnp.concatenate([cos, cos], axis=-1), jnp.concatenate([sin, sin], axis=-1)


def kernel(x_prompt, x_sample, cache_ckv, cache_kpe, state_pool, attn_norm, w_in, q_a_norm, w_q_b, kv_a_norm, w_kv_b, q_norm, k_norm, w_pool, pool_scale, w_out, ffn_norm, w_router, b_router, w_gate, b_gate, w_up, b_up, w_down, b_down):
    bp, seq, d = x_prompt.shape
    nb, t_dec, _ = x_sample.shape
    past = cache_ckv.shape[1]
    kv_lora = cache_ckv.shape[2]
    d_pool = state_pool.shape[2]
    q_lora = q_a_norm.shape[0]
    nh = w_q_b.shape[1] // QK_HEAD
    ne = w_router.shape[1]
    assert bp == 1 and seq >= POOL_STATE and t_dec >= POOL_STATE
    assert w_in.shape[1] == d_pool + q_lora + kv_lora + QK_ROPE

    n_p, n_s = bp * seq, nb * t_dec
    n = n_p + n_s
    tm = _tile(math.gcd(n_p, n_s), 512)
    xa, xb = x_prompt.reshape(n_p, d), x_sample.reshape(n_s, d)

    cos_p, sin_p = _rope_tables(jnp.arange(seq))
    cos_s, sin_s = _rope_tables(past + jnp.arange(t_dec))
    cos2 = jnp.concatenate([cos_p] * bp + [cos_s] * nb, axis=0)
    sin2 = jnp.concatenate([sin_p] * bp + [sin_s] * nb, axis=0)

    tn_in = _tile(kv_lora, 512, 128)
    while d_pool % tn_in or q_lora % tn_in:
        tn_in //= 2
    assert tn_in >= 2 * QK_ROPE
    w_kpe = w_in[:, d_pool + q_lora + kv_lora:]
    w_all = jnp.concatenate(
        [w_in[:, :d_pool + q_lora + kv_lora], w_kpe, _rotate_half_cols(w_kpe),
         jnp.zeros((d, tn_in - 2 * QK_ROPE), w_in.dtype)], axis=1).astype(BF16)
    wq = w_q_b.reshape(q_lora, nh, QK_HEAD)
    wq = jnp.concatenate([wq, _rotate_half_cols(wq[..., QK_NOPE:])], axis=-1)
    wq = wq.transpose(1, 0, 2).astype(BF16)
    wkv = w_kv_b.reshape(kv_lora, nh, QK_NOPE + V_HEAD).transpose(1, 0, 2).astype(BF16)
    w_out_b = w_out.astype(BF16)

    u, cqn, ckv, kpe = _inproj(xa, xb, attn_norm.reshape(1, d), w_all, q_a_norm.reshape(1, q_lora),
                               kv_a_norm.reshape(1, kv_lora), cos2, sin2,
                               d_pool=d_pool, q_lora=q_lora, kv_lora=kv_lora, tn=tn_in, tm=tm)
    hb = 4 if nh % 4 == 0 else (2 if nh % 2 == 0 else 1)
    q = _qproj(cqn, wq, q_norm.reshape(1, QK_HEAD), cos2, sin2, tm=tm, hb=hb)
    t_att = _tile(math.gcd(seq, n), 512, CHUNK)
    k_new, v_new, vt_new = _kvexp(ckv, kpe, wkv, k_norm.reshape(1, QK_HEAD), tm=t_att, hb=hb, with_vt=True)
    n_c = nb * past
    k_past, v_past = _kvexp(cache_ckv.reshape(n_c, kv_lora), cache_kpe.reshape(n_c, QK_ROPE), wkv,
                            k_norm.reshape(1, QK_HEAD), tm=_tile(n_c, 512), hb=hb, with_vt=False)

    attn_p = _attn_prompt(q, k_new, vt_new, seq=seq, tk=t_att)
    attn_s = _attn_sample(q, k_new, v_new, k_past, v_past, nb=nb, t=t_dec, past=past, row0=n_p, hb=hb)

    u_p = u[:n_p].reshape(bp, seq, d_pool)
    u_s = u[n_p:].reshape(nb, t_dec, d_pool)
    w_pool_b = w_pool.astype(BF16)
    scale2 = pool_scale.reshape(1, d_pool)
    halo_p = jnp.zeros((bp, POOL_HALO, d_pool), F32)
    halo_s = jnp.concatenate([jnp.zeros((nb, 1, d_pool), F32), state_pool], axis=1)
    pool_p = _pool(u_p, halo_p, w_pool_b, scale2, pos0=0, tm=_tile(seq, 512))
    pool_s = _pool(u_s, halo_s, w_pool_b, scale2, pos0=past, tm=t_dec)

    x1 = _outproj(pool_p, pool_s, attn_p, attn_s, w_out_b[:d_pool], w_out_b[d_pool:], xa, xb,
                  tm=tm, tn=_tile(d, 1024, 128))

    hp, top_idx, gates, rank, counts = _router(x1, ffn_norm.reshape(1, d), w_router.T.astype(BF16),
                                               b_router.reshape(ne, 1), tm=tm)
    tm_moe = 512
    counts = counts.reshape(ne)
    padded = (counts + tm_moe - 1) // tm_moe * tm_moe
    pend = jnp.cumsum(padded)
    pstart = pend - padded
    experts = jnp.arange(ne, dtype=jnp.int32)
    first_slot = jnp.sum(jnp.where(top_idx[..., None] == experts, pstart.astype(jnp.int32), 0), axis=-1)
    slot_flat = (first_slot + rank).reshape(TOP_K * n).astype(jnp.int32)
    n_blocks = -(-(n * TOP_K) // tm_moe) + ne
    block_row0 = jnp.arange(n_blocks, dtype=jnp.int32) * tm_moe
    block_expert = jnp.minimum(jnp.sum(block_row0[:, None] >= pend[None, :], axis=1), ne - 1).astype(jnp.int32)
    n_used = (pend[-1:] // tm_moe).astype(jnp.int32)

    xb = _dispatch(slot_flat, (pstart + counts).astype(jnp.int32), (padded - counts).astype(jnp.int32), n_used, hp,
                   tm=tm_moe, nblk=n_blocks, tt=_tile(n, 256, 8))
    dff = w_gate.shape[2]
    h1 = _gateup(block_expert, n_used, xb, w_gate, w_up, b_gate, b_up, tm=tm_moe, tn=_tile(dff, 512, 128))
    yb = _down(block_expert, n_used, h1, w_down, b_down, tm=tm_moe, tn=_tile(d, 1024, 128))
    y_p, y_s = _combine(slot_flat, x1, gates.T, yb, n_first=n_p, tc=_tile(math.gcd(n_p, n_s), 128, 8))

    return (
        y_p.reshape(bp, seq, d),
        y_s.reshape(nb, t_dec, d),
        ckv[:n_p].reshape(bp, seq, kv_lora),
        kpe[:n_p].reshape(bp, seq, QK_ROPE),
        u_p[:, seq - POOL_STATE:],
        ckv[n_p:].reshape(nb, t_dec, kv_lora),
        kpe[n_p:].reshape(nb, t_dec, QK_ROPE),
        u_s[:, t_dec - POOL_STATE:],
    )
```

```python
import functools
import math

import jax
import jax.numpy as jnp
from jax import lax
from jax.experimental import pallas as pl
from jax.experimental.pallas import tpu as pltpu

CHUNK = 64
POOL_WINDOWS = (2, 4, 8, 16)
POOL_STATE = max(POOL_WINDOWS) - 1
POOL_HALO = POOL_STATE + 1
QK_NOPE = 128
QK_ROPE = 64
QK_HEAD = QK_NOPE + QK_ROPE
V_HEAD = 128
ROPE_THETA = 10000.0
TOP_K = 4
SWIGLU_LIMIT = 7.0
SWIGLU_ALPHA = 1.702
EPS = 1e-6
LOG2_E = 1.4426950408889634

F32 = jnp.float32
BF16 = jnp.bfloat16
MIB = 1024 * 1024
NT_DIMS = (((1,), (1,)), ((), ()))


def _params(semantics, vmem_mib=None):
    kw = dict(dimension_semantics=semantics)
    if vmem_mib is not None:
        kw["vmem_limit_bytes"] = vmem_mib * MIB
    return pltpu.CompilerParams(**kw)


def _tile(n, pref, mult=16):
    t = min(pref, n)
    t -= t % mult
    while t > mult and n % t:
        t -= mult
    assert t >= mult and n % t == 0, (n, pref, mult)
    return t


def _inproj_kernel(xa_ref, xb_ref, g_ref, w_ref, qg_ref, kvg_ref, cos_ref, sin_ref,
                   u_ref, cqn_ref, ckv_ref, kpe_ref, h_sc, seg_sc, *, nu, nq, nkv, tn, na):
    i = pl.program_id(0)
    j = pl.program_id(1)

    def normalise(x_ref):
        x = x_ref[...]
        ms = jnp.mean(x * x, axis=-1, keepdims=True)
        h_sc[...] = (x * lax.rsqrt(ms + EPS) * g_ref[...]).astype(BF16)

    @pl.when(jnp.logical_and(j == 0, i < na))
    def _():
        normalise(xa_ref)

    @pl.when(jnp.logical_and(j == 0, i >= na))
    def _():
        normalise(xb_ref)

    y = jnp.dot(h_sc[...], w_ref[...], preferred_element_type=F32)

    @pl.when(j < nu)
    def _():
        u_ref[...] = y

    def finish(n, gain_ref, out_ref):
        parts = [seg_sc[t] for t in range(n - 1)] + [y]
        ss = jnp.sum(parts[0] * parts[0], axis=-1, keepdims=True)
        for p in parts[1:]:
            ss = ss + jnp.sum(p * p, axis=-1, keepdims=True)
        r = lax.rsqrt(ss / (n * tn) + EPS)
        for t, p in enumerate(parts):
            out_ref[:, t * tn:(t + 1) * tn] = (p * r * gain_ref[:, t * tn:(t + 1) * tn]).astype(out_ref.dtype)

    def segment(first, n, gain_ref, out_ref):
        for t in range(n - 1):
            @pl.when(j == first + t)
            def _(t=t):
                seg_sc[t] = y

        @pl.when(j == first + n - 1)
        def _():
            finish(n, gain_ref, out_ref)

    segment(nu, nq, qg_ref, cqn_ref)
    segment(nu + nq, nkv, kvg_ref, ckv_ref)

    @pl.when(j == nu + nq + nkv)
    def _():
        kpe_ref[...] = y[:, :QK_ROPE] * cos_ref[...] + y[:, QK_ROPE:2 * QK_ROPE] * sin_ref[...]


def _two_group_maps(na):
    return (lambda i: jnp.minimum(i, na - 1)), (lambda i: jnp.maximum(i - na, 0))


def _inproj(xa, xb, attn_norm, w_all, q_a_norm, kv_a_norm, cos2, sin2, *, d_pool, q_lora, kv_lora, tn, tm):
    d = xa.shape[1]
    assert xa.shape[0] % tm == 0 and xb.shape[0] % tm == 0
    na = xa.shape[0] // tm
    n = xa.shape[0] + xb.shape[0]
    amap, bmap = _two_group_maps(na)
    nu, nq, nkv = d_pool // tn, q_lora // tn, kv_lora // tn
    nj = nu + nq + nkv + 1
    assert w_all.shape == (d, nj * tn)
    kern = functools.partial(_inproj_kernel, nu=nu, nq=nq, nkv=nkv, tn=tn, na=na)
    return pl.pallas_call(
        kern,
        grid=(n // tm, nj),
        in_specs=[
            pl.BlockSpec((tm, d), lambda i, j: (amap(i), 0)),
            pl.BlockSpec((tm, d), lambda i, j: (bmap(i), 0), pipeline_mode=pl.Buffered(1)),
            pl.BlockSpec((1, d), lambda i, j: (0, 0)),
            pl.BlockSpec((d, tn), lambda i, j: (0, j)),
            pl.BlockSpec((1, q_lora), lambda i, j: (0, 0)),
            pl.BlockSpec((1, kv_lora), lambda i, j: (0, 0)),
            pl.BlockSpec((tm, QK_ROPE), lambda i, j: (i, 0)),
            pl.BlockSpec((tm, QK_ROPE), lambda i, j: (i, 0)),
        ],
        out_specs=[
            pl.BlockSpec((tm, tn), lambda i, j: (i, jnp.minimum(j, nu - 1))),
            pl.BlockSpec((tm, q_lora), lambda i, j: (i, 0)),
            pl.BlockSpec((tm, kv_lora), lambda i, j: (i, 0)),
            pl.BlockSpec((tm, QK_ROPE), lambda i, j: (i, 0)),
        ],
        out_shape=[
            jax.ShapeDtypeStruct((n, d_pool), F32),
            jax.ShapeDtypeStruct((n, q_lora), BF16),
            jax.ShapeDtypeStruct((n, kv_lora), F32),
            jax.ShapeDtypeStruct((n, QK_ROPE), F32),
        ],
        scratch_shapes=[
            pltpu.VMEM((tm, d), BF16),
            pltpu.VMEM((max(nq, nkv, 2) - 1, tm, tn), F32),
        ],
        compiler_params=_params(("parallel", "arbitrary"), 58),
        name="inproj",
    )(xa, xb, attn_norm, w_all, q_a_norm, kv_a_norm, cos2, sin2)


def _qproj_kernel(c_ref, w_ref, g_ref, cos_ref, sin_ref, q_ref, *, hb, scale):
    c = c_ref[...]
    cos = cos_ref[...]
    sin = sin_ref[...]
    g = g_ref[...]
    for h in range(hb):
        y = jnp.dot(c, w_ref[h], preferred_element_type=F32)
        nope = y[:, :QK_NOPE]
        rope = y[:, QK_NOPE:QK_HEAD] * cos + y[:, QK_HEAD:QK_HEAD + QK_ROPE] * sin
        ss = jnp.sum(nope * nope, axis=-1, keepdims=True) + jnp.sum(rope * rope, axis=-1, keepdims=True)
        r = lax.rsqrt(ss / QK_HEAD + EPS) * scale
        q_ref[h, :, :QK_NOPE] = (nope * r * g[:, :QK_NOPE]).astype(BF16)
        q_ref[h, :, QK_NOPE:] = (rope * r * g[:, QK_NOPE:]).astype(BF16)


def _qproj(cqn, wq, q_norm, cos2, sin2, *, tm, hb):
    n, q_lora = cqn.shape
    nh = wq.shape[0]
    kern = functools.partial(_qproj_kernel, hb=hb, scale=QK_HEAD ** -0.5 * LOG2_E)
    return pl.pallas_call(
        kern,
        grid=(n // tm, nh // hb),
        in_specs=[
            pl.BlockSpec((tm, q_lora), lambda i, j: (i, 0)),
            pl.BlockSpec((hb, q_lora, QK_HEAD + QK_ROPE), lambda i, j: (j, 0, 0)),
            pl.BlockSpec((1, QK_HEAD), lambda i, j: (0, 0)),
            pl.BlockSpec((tm, QK_ROPE), lambda i, j: (i, 0)),
            pl.BlockSpec((tm, QK_ROPE), lambda i, j: (i, 0)),
        ],
        out_specs=pl.BlockSpec((hb, tm, QK_HEAD), lambda i, j: (j, i, 0)),
        out_shape=jax.ShapeDtypeStruct((nh, n, QK_HEAD), BF16),
        compiler_params=_params(("parallel", "arbitrary"), 48),
        name="qproj",
    )(cqn, wq, q_norm, cos2, sin2)


def _kvexp_kernel(c_ref, pe_ref, w_ref, g_ref, k_ref, v_ref, *maybe_vt_ref, hb):
    c = c_ref[...].astype(BF16)
    pe = pe_ref[...]
    g = g_ref[...]
    pe_ss = jnp.sum(pe * pe, axis=-1, keepdims=True)
    for h in range(hb):
        y = jnp.dot(c, w_ref[h], preferred_element_type=F32)
        nope = y[:, :QK_NOPE]
        r = lax.rsqrt((jnp.sum(nope * nope, axis=-1, keepdims=True) + pe_ss) / QK_HEAD + EPS)
        k_ref[h, :, :QK_NOPE] = (nope * r * g[:, :QK_NOPE]).astype(BF16)
        k_ref[h, :, QK_NOPE:] = (pe * r * g[:, QK_NOPE:]).astype(BF16)
        v = y[:, QK_NOPE:]
        v_ref[h] = v.astype(BF16)
        for vt_ref in maybe_vt_ref:
            vt_ref[h, 0] = v.T.astype(BF16)


def _kvexp(ckv, kpe, wkv, k_norm, *, tm, hb, with_vt):
    n, kv_lora = ckv.shape
    nh = wkv.shape[0]
    kern = functools.partial(_kvexp_kernel, hb=hb)
    vt_spec = [pl.BlockSpec((hb, 1, V_HEAD, tm), lambda i, j: (j, i, 0, 0))] if with_vt else []
    vt_shape = [jax.ShapeDtypeStruct((nh, n // tm, V_HEAD, tm), BF16)] if with_vt else []
    return pl.pallas_call(
        kern,
        grid=(n // tm, nh // hb),
        in_specs=[
            pl.BlockSpec((tm, kv_lora), lambda i, j: (i, 0)),
            pl.BlockSpec((tm, QK_ROPE), lambda i, j: (i, 0)),
            pl.BlockSpec((hb, kv_lora, QK_NOPE + V_HEAD), lambda i, j: (j, 0, 0)),
            pl.BlockSpec((1, QK_HEAD), lambda i, j: (0, 0)),
        ],
        out_specs=[
            pl.BlockSpec((hb, tm, QK_HEAD), lambda i, j: (j, i, 0)),
            pl.BlockSpec((hb, tm, V_HEAD), lambda i, j: (j, i, 0)),
        ] + vt_spec,
        out_shape=[
            jax.ShapeDtypeStruct((nh, n, QK_HEAD), BF16),
            jax.ShapeDtypeStruct((nh, n, V_HEAD), BF16),
        ] + vt_shape,
        compiler_params=_params(("parallel", "arbitrary"), 48),
        name="kvexp",
    )(ckv, kpe, wkv, k_norm)


def _attn_prompt_kernel(q_ref, k_ref, vt_ref, o_ref, qt_sc, st_sc, m_sc, l_sc, acc_sc, *, tk):
    i = pl.program_id(1)
    m_sc[...] = jnp.full(m_sc.shape, -jnp.inf, F32)
    l_sc[...] = jnp.zeros(l_sc.shape, F32)
    acc_sc[...] = jnp.zeros(acc_sc.shape, F32)
    eye = (lax.broadcasted_iota(jnp.int32, (QK_HEAD, QK_HEAD), 0)
           == lax.broadcasted_iota(jnp.int32, (QK_HEAD, QK_HEAD), 1)).astype(BF16)
    qt_sc[...] = lax.dot_general(eye, q_ref[0], NT_DIMS, preferred_element_type=F32).astype(BF16)

    def qk(c, slot, q0=0):
        off = pl.multiple_of(c * tk, tk)
        st_sc[slot, :, q0:] = jnp.dot(k_ref[0, pl.ds(off, tk), :], qt_sc[:, q0:], preferred_element_type=F32)

    def update(c, slot, q0=0, diagonal=False):
        st = st_sc[slot, :, q0:]
        if diagonal:
            key_chunk = lax.broadcasted_iota(jnp.int32, st.shape, 0) // CHUNK
            qry_chunk = lax.broadcasted_iota(jnp.int32, st.shape, 1) // CHUNK
            st = jnp.where(key_chunk <= qry_chunk, st, -jnp.inf)
        m_old = m_sc[:, q0:]
        m_new = jnp.maximum(m_old, jnp.max(st, axis=0, keepdims=True))
        alpha = jnp.exp2(m_old - m_new)
        p = jnp.exp2(st - m_new)
        l_sc[:, q0:] = alpha * l_sc[:, q0:] + jnp.sum(p, axis=0, keepdims=True)
        acc_sc[:, q0:] = alpha * acc_sc[:, q0:] + jnp.dot(vt_ref[0, c], p.astype(BF16),
                                                          preferred_element_type=F32)
        m_sc[:, q0:] = m_new

    qk(0, 0)

    def pair(j, carry):
        c = 2 * j
        qk(c + 1, 1)
        update(c, 0)
        qk(c + 2, 0)
        update(c + 1, 1)
        return carry

    lax.fori_loop(0, i, pair, 0)
    qk(2 * i + 1, 1, q0=tk)
    update(2 * i, 0, diagonal=True)
    update(2 * i + 1, 1, q0=tk, diagonal=True)

    o_ref[...] = (acc_sc[...] / l_sc[...]).T.astype(o_ref.dtype)


def _attn_prompt(q, k, vt, *, seq, tk):
    nh = q.shape[0]
    tq = 2 * tk
    assert seq % tq == 0 and tk % CHUNK == 0 and vt.shape[3] == tk
    kern = functools.partial(_attn_prompt_kernel, tk=tk)
    return pl.pallas_call(
        kern,
        grid=(nh, seq // tq),
        in_specs=[
            pl.BlockSpec((1, tq, QK_HEAD), lambda h, i: (h, i, 0)),
            pl.BlockSpec((1, seq, QK_HEAD), lambda h, i: (h, 0, 0)),
            pl.BlockSpec((1, seq // tk, V_HEAD, tk), lambda h, i: (h, 0, 0, 0)),
        ],
        out_specs=pl.BlockSpec((tq, V_HEAD), lambda h, i: (i, h)),
        out_shape=jax.ShapeDtypeStruct((seq, nh * V_HEAD), BF16),
        scratch_shapes=[
            pltpu.VMEM((QK_HEAD, tq), BF16),
            pltpu.VMEM((2, tk, tq), F32),
            pltpu.VMEM((1, tq), F32),
            pltpu.VMEM((1, tq), F32),
            pltpu.VMEM((V_HEAD, tq), F32),
        ],
        compiler_params=_params(("parallel", "arbitrary"), 48),
        name="attn_prompt",
    )(q, k, vt)


def _attn_sample_kernel(q_ref, kp_ref, vp_ref, kn_ref, vn_ref, o_ref, *, hb):
    for h in range(hb):
        q = q_ref[h]
        s1 = lax.dot_general(q, kp_ref[h], NT_DIMS, preferred_element_type=F32)
        s2 = lax.dot_general(q, kn_ref[h], NT_DIMS, preferred_element_type=F32)
        m = jnp.maximum(jnp.max(s1, axis=-1, keepdims=True), jnp.max(s2, axis=-1, keepdims=True))
        p1 = jnp.exp2(s1 - m)
        p2 = jnp.exp2(s2 - m)
        l = jnp.sum(p1, axis=-1, keepdims=True) + jnp.sum(p2, axis=-1, keepdims=True)
        o = (jnp.dot(p1.astype(BF16), vp_ref[h], preferred_element_type=F32)
             + jnp.dot(p2.astype(BF16), vn_ref[h], preferred_element_type=F32))
        o_ref[:, h * V_HEAD:(h + 1) * V_HEAD] = (o / l).astype(o_ref.dtype)


def _attn_sample(q, k_new, v_new, k_past, v_past, *, nb, t, past, row0, hb):
    nh = q.shape[0]
    assert row0 % t == 0 and nh % hb == 0
    r0 = row0 // t
    return pl.pallas_call(
        functools.partial(_attn_sample_kernel, hb=hb),
        grid=(nh // hb, nb),
        in_specs=[
            pl.BlockSpec((hb, t, QK_HEAD), lambda h, b: (h, r0 + b, 0)),
            pl.BlockSpec((hb, past, QK_HEAD), lambda h, b: (h, b, 0)),
            pl.BlockSpec((hb, past, V_HEAD), lambda h, b: (h, b, 0)),
            pl.BlockSpec((hb, t, QK_HEAD), lambda h, b: (h, r0 + b, 0)),
            pl.BlockSpec((hb, t, V_HEAD), lambda h, b: (h, r0 + b, 0)),
        ],
        out_specs=pl.BlockSpec((t, hb * V_HEAD), lambda h, b: (b, h)),
        out_shape=jax.ShapeDtypeStruct((nb * t, nh * V_HEAD), BF16),
        compiler_params=_params(("parallel", "arbitrary"), 48),
        name="attn_sample",
    )(q, k_past, v_past, k_new, v_new)


def _pool_kernel(u_ref, prev_ref, w_ref, sc_ref, o_ref, ext_sc, *, tm, pg, pos0):
    t = pl.program_id(1)

    @pl.when(t == 0)
    def _():
        ext_sc[0:POOL_HALO, :] = prev_ref[0]

    @pl.when(t > 0)
    def _():
        ext_sc[0:POOL_HALO, :] = ext_sc[tm:tm + POOL_HALO, :]

    u = u_ref[0]
    ext_sc[POOL_HALO:, :] = u
    pos = pos0 + t * tm + lax.broadcasted_iota(jnp.int32, (tm, 1), 0)
    for g, w in enumerate(POOL_WINDOWS):
        cols = slice(g * pg, (g + 1) * pg)
        win = ext_sc[POOL_HALO:POOL_HALO + tm, cols]
        for back in range(1, w):
            win = win + ext_sc[POOL_HALO - back:POOL_HALO - back + tm, cols]
        inv_cnt = 1.0 / jnp.minimum(pos + 1, w).astype(F32)
        pooled = win * inv_cnt - u[:, cols]
        y = jnp.dot(pooled.astype(BF16), w_ref[g], preferred_element_type=F32)
        o_ref[:, cols] = (y * sc_ref[:, cols]).astype(o_ref.dtype)


def _pool(u, prev_halo, w_pool, pool_scale, *, pos0, tm):
    b, t, c = u.shape
    pg = c // len(POOL_WINDOWS)
    nt = t // tm
    kern = functools.partial(_pool_kernel, tm=tm, pg=pg, pos0=pos0)
    return pl.pallas_call(
        kern,
        grid=(b, nt),
        in_specs=[
            pl.BlockSpec((1, tm, c), lambda bi, ti: (bi, ti, 0)),
            pl.BlockSpec((1, POOL_HALO, c), lambda bi, ti: (bi, 0, 0)),
            pl.BlockSpec((len(POOL_WINDOWS), pg, pg), lambda bi, ti: (0, 0, 0)),
            pl.BlockSpec((1, c), lambda bi, ti: (0, 0)),
        ],
        out_specs=pl.BlockSpec((tm, c), lambda bi, ti: (bi * nt + ti, 0)),
        out_shape=jax.ShapeDtypeStruct((b * t, c), BF16),
        scratch_shapes=[pltpu.VMEM((tm + POOL_HALO, c), F32)],
        compiler_params=_params(("arbitrary", "arbitrary"), 48),
        name="pool",
    )(u, prev_halo, w_pool, pool_scale)


def _outproj_kernel(pa_ref, pb_ref, aa_ref, ab_ref, wp_ref, wa_ref, xa_ref, xb_ref, o_ref, *, na):
    i = pl.program_id(1)

    def run(p_ref, a_ref, x_ref):
        acc = jnp.dot(p_ref[...], wp_ref[...], preferred_element_type=F32)
        acc = acc + jnp.dot(a_ref[...], wa_ref[...], preferred_element_type=F32)
        o_ref[...] = x_ref[...] + acc

    @pl.when(i < na)
    def _():
        run(pa_ref, aa_ref, xa_ref)

    @pl.when(i >= na)
    def _():
        run(pb_ref, ab_ref, xb_ref)


def _outproj(pool_a, pool_b, attn_a, attn_b, w_p, w_a, xa, xb, *, tm, tn):
    d = xa.shape[1]
    assert xa.shape[0] % tm == 0 and xb.shape[0] % tm == 0
    na = xa.shape[0] // tm
    n = xa.shape[0] + xb.shape[0]
    amap, bmap = _two_group_maps(na)
    dp, da = pool_a.shape[1], attn_a.shape[1]
    return pl.pallas_call(
        functools.partial(_outproj_kernel, na=na),
        grid=(d // tn, n // tm),
        in_specs=[
            pl.BlockSpec((tm, dp), lambda j, i: (amap(i), 0)),
            pl.BlockSpec((tm, dp), lambda j, i: (bmap(i), 0)),
            pl.BlockSpec((tm, da), lambda j, i: (amap(i), 0)),
            pl.BlockSpec((tm, da), lambda j, i: (bmap(i), 0)),
            pl.BlockSpec((dp, tn), lambda j, i: (0, j)),
            pl.BlockSpec((da, tn), lambda j, i: (0, j)),
            pl.BlockSpec((tm, tn), lambda j, i: (amap(i), j)),
            pl.BlockSpec((tm, tn), lambda j, i: (bmap(i), j)),
        ],
        out_specs=pl.BlockSpec((tm, tn), lambda j, i: (i, j)),
        out_shape=jax.ShapeDtypeStruct((n, d), F32),
        compiler_params=_params(("parallel", "arbitrary"), 48),
        name="outproj",
    )(pool_a, pool_b, attn_a, attn_b, w_p, w_a, xa, xb)


def _router_kernel(x_ref, g_ref, wr_ref, br_ref, hp_ref, idx_ref, gate_ref, rank_ref, cnt_ref, run_sc, *, ne):
    i = pl.program_id(0)

    @pl.when(i == 0)
    def _():
        run_sc[...] = jnp.zeros(run_sc.shape, F32)

    x = x_ref[...]
    tm, d = x.shape
    ms = jnp.mean(x * x, axis=-1, keepdims=True)
    h = x * lax.rsqrt(ms + EPS) * g_ref[...]
    hb = h.astype(BF16)

    bits = lax.bitcast_convert_type(hb.astype(F32), jnp.uint32)
    hp_ref[...] = (bits[:, :d // 2] >> 16) | (bits[:, d // 2:] & jnp.uint32(0xFFFF0000))

    logits = lax.dot_general(wr_ref[...], hb, NT_DIMS, preferred_element_type=F32) + br_ref[...]
    eidx = lax.broadcasted_iota(jnp.int32, logits.shape, 0)
    work = logits
    vals, idxs = [], []
    for _ in range(TOP_K):
        m = jnp.max(work, axis=0, keepdims=True)
        sel = jnp.min(jnp.where(work == m, eidx, ne), axis=0, keepdims=True)
        vals.append(m)
        idxs.append(sel)
        work = jnp.where(eidx == sel, -jnp.inf, work)
    exps = [jnp.exp(v - vals[0]) for v in vals]
    den = exps[0]
    for e in exps[1:]:
        den = den + e

    hits = [(eidx == sel).astype(F32) for sel in idxs]
    onehot = hits[0]
    for hit in hits[1:]:
        onehot = onehot + hit
    before = (lax.broadcasted_iota(jnp.int32, (tm, tm), 0) < lax.broadcasted_iota(jnp.int32, (tm, tm), 1))
    prefix = jnp.dot(onehot.astype(BF16), before.astype(BF16), preferred_element_type=F32)
    base = prefix + run_sc[...]
    for k in range(TOP_K):
        idx_ref[k:k + 1, :] = idxs[k]
        gate_ref[k:k + 1, :] = exps[k] / den
        rank_ref[k:k + 1, :] = jnp.sum(hits[k] * base, axis=0, keepdims=True).astype(jnp.int32)
    run_sc[...] = run_sc[...] + jnp.sum(onehot, axis=1, keepdims=True)
    cnt_ref[...] = run_sc[...].astype(jnp.int32)


def _router(x1, ffn_norm, wr_t, b_router, *, tm):
    n, d = x1.shape
    ne = wr_t.shape[0]
    kern = functools.partial(_router_kernel, ne=ne)
    return pl.pallas_call(
        kern,
        grid=(n // tm,),
        in_specs=[
            pl.BlockSpec((tm, d), lambda i: (i, 0)),
            pl.BlockSpec((1, d), lambda i: (0, 0)),
            pl.BlockSpec((ne, d), lambda i: (0, 0)),
            pl.BlockSpec((ne, 1), lambda i: (0, 0)),
        ],
        out_specs=[
            pl.BlockSpec((tm, d // 2), lambda i: (i, 0)),
            pl.BlockSpec((TOP_K, tm), lambda i: (0, i)),
            pl.BlockSpec((TOP_K, tm), lambda i: (0, i)),
            pl.BlockSpec((TOP_K, tm), lambda i: (0, i)),
            pl.BlockSpec((ne, 1), lambda i: (0, 0)),
        ],
        out_shape=[
            jax.ShapeDtypeStruct((n, d // 2), jnp.uint32),
            jax.ShapeDtypeStruct((TOP_K, n), jnp.int32),
            jax.ShapeDtypeStruct((TOP_K, n), F32),
            jax.ShapeDtypeStruct((TOP_K, n), jnp.int32),
            jax.ShapeDtypeStruct((ne, 1), jnp.int32),
        ],
        scratch_shapes=[pltpu.VMEM((ne, 1), F32)],
        compiler_params=_params(("arbitrary",), 48),
        name="router",
    )(x1, ffn_norm, wr_t, b_router)


def _dispatch_kernel(slot_ref, pad0_ref, padn_ref, nused_ref, h_ref, xb_ref, zero_sc, sem, *, tt, n, ne, tm, nblk):
    i = pl.program_id(0)
    zr = zero_sc.shape[0]

    def row_copy(src, r, s):
        return pltpu.make_async_copy(src.at[pl.ds(r, 1)], xb_ref.at[pl.ds(s, 1)], sem)

    def issue(r, carry):
        for k in range(TOP_K):
            row_copy(h_ref, r, slot_ref[k * n + i * tt + r]).start()
        return carry

    def drain(r, carry):
        for k in range(TOP_K):
            row_copy(h_ref, 0, 0).wait()
        return carry

    lax.fori_loop(0, tt, issue, 0)
    lax.fori_loop(0, tt, drain, 0)

    @pl.when(i == pl.num_programs(0) - 1)
    def _():
        zero_sc[...] = jnp.zeros(zero_sc.shape, zero_sc.dtype)

        def per_expert(e, carry):
            start = pad0_ref[e]
            cnt = padn_ref[e]
            lax.fori_loop(0, cnt, lambda r, c: (row_copy(zero_sc, 0, start + r).start(), c)[1], 0)
            lax.fori_loop(0, cnt, lambda r, c: (row_copy(zero_sc, 0, 0).wait(), c)[1], 0)
            return carry

        lax.fori_loop(0, ne, per_expert, 0)

        def strip_copy(blk, c):
            return pltpu.make_async_copy(zero_sc, xb_ref.at[pl.ds(blk * tm + c * zr, zr)], sem)

        def per_block(blk, carry):
            for c in range(tm // zr):
                strip_copy(blk, c).start()
            for c in range(tm // zr):
                strip_copy(blk, c).wait()
            return carry

        lax.fori_loop(nused_ref[0], nblk, per_block, 0)


def _dispatch(slot_flat, pad_start, pad_count, n_used, hp, *, tm, nblk, tt):
    n, half = hp.shape
    ne = pad_start.shape[0]
    zr = _tile(tm, 64, 8)
    kern = functools.partial(_dispatch_kernel, tt=tt, n=n, ne=ne, tm=tm, nblk=nblk)
    return pl.pallas_call(
        kern,
        grid_spec=pltpu.PrefetchScalarGridSpec(
            num_scalar_prefetch=4,
            grid=(n // tt,),
            in_specs=[pl.BlockSpec((tt, half), lambda i, *_: (i, 0))],
            out_specs=pl.BlockSpec(memory_space=pl.ANY),
            scratch_shapes=[pltpu.VMEM((zr, half), jnp.uint32), pltpu.SemaphoreType.DMA(())],
        ),
        out_shape=jax.ShapeDtypeStruct((nblk * tm, half), jnp.uint32),
        compiler_params=_params(("arbitrary",), 48),
        name="dispatch",
    )(slot_flat, pad_start, pad_count, n_used, hp)


def _unpack_rows(p):
    lo = lax.bitcast_convert_type(p << 16, F32).astype(BF16)
    hi = lax.bitcast_convert_type(p & jnp.uint32(0xFFFF0000), F32).astype(BF16)
    return lo, hi


def _gateup_kernel(be_ref, nused_ref, x_ref, wg_ref, wu_ref, bg_ref, bu_ref, h1_ref):
    i = pl.program_id(1)

    @pl.when(i < nused_ref[0])
    def _():
        lo, hi = _unpack_rows(x_ref[...])
        half = lo.shape[1]

        def proj(w_ref, b_ref):
            return (jnp.dot(lo, w_ref[0, :half, :].astype(BF16), preferred_element_type=F32)
                    + jnp.dot(hi, w_ref[0, half:, :].astype(BF16), preferred_element_type=F32) + b_ref[0])

        g = jnp.minimum(proj(wg_ref, bg_ref), SWIGLU_LIMIT)
        up = jnp.clip(proj(wu_ref, bu_ref), -SWIGLU_LIMIT, SWIGLU_LIMIT)
        h1_ref[...] = (g * jax.nn.sigmoid(SWIGLU_ALPHA * g) * (up + 1.0)).astype(h1_ref.dtype)

    @pl.when(i >= nused_ref[0])
    def _():
        h1_ref[...] = jnp.zeros(h1_ref.shape, h1_ref.dtype)


def _gateup(block_expert, n_used, xb, w_gate, w_up, b_gate, b_up, *, tm, tn):
    rows, half = xb.shape
    ne, d, dff = w_gate.shape
    nblk = rows // tm

    def xmap(j, i, be, nu):
        return (jnp.minimum(i, nu[0] - 1), 0)

    def wmap(j, i, be, nu):
        return (be[i], 0, j)

    return pl.pallas_call(
        _gateup_kernel,
        grid_spec=pltpu.PrefetchScalarGridSpec(
            num_scalar_prefetch=2,
            grid=(dff // tn, nblk),
            in_specs=[
                pl.BlockSpec((tm, half), xmap),
                pl.BlockSpec((1, d, tn), wmap),
                pl.BlockSpec((1, d, tn), wmap),
                pl.BlockSpec((1, 1, tn), wmap),
                pl.BlockSpec((1, 1, tn), wmap),
            ],
            out_specs=pl.BlockSpec((tm, tn), lambda j, i, be, nu: (i, j)),
        ),
        out_shape=jax.ShapeDtypeStruct((rows, dff), BF16),
        compiler_params=_params(("arbitrary", "arbitrary"), 60),
        name="moe_gateup",
    )(block_expert, n_used, xb, w_gate, w_up, b_gate.reshape(ne, 1, dff), b_up.reshape(ne, 1, dff))


def _down_kernel(be_ref, nused_ref, h1_ref, wd_ref, bd_ref, y_ref):
    i = pl.program_id(1)

    @pl.when(i < nused_ref[0])
    def _():
        y_ref[...] = jnp.dot(h1_ref[...], wd_ref[0].astype(BF16), preferred_element_type=F32) + bd_ref[0]

    @pl.when(i >= nused_ref[0])
    def _():
        y_ref[...] = jnp.zeros(y_ref.shape, y_ref.dtype)


def _down(block_expert, n_used, h1, w_down, b_down, *, tm, tn):
    rows, dff = h1.shape
    ne, _, d = w_down.shape
    nblk = rows // tm

    def wmap(j, i, be, nu):
        return (be[i], 0, j)

    return pl.pallas_call(
        _down_kernel,
        grid_spec=pltpu.PrefetchScalarGridSpec(
            num_scalar_prefetch=2,
            grid=(d // tn, nblk),
            in_specs=[
                pl.BlockSpec((tm, dff), lambda j, i, be, nu: (jnp.minimum(i, nu[0] - 1), 0)),
                pl.BlockSpec((1, dff, tn), wmap),
                pl.BlockSpec((1, 1, tn), wmap),
            ],
            out_specs=pl.BlockSpec((tm, tn), lambda j, i, be, nu: (i, j)),
        ),
        out_shape=jax.ShapeDtypeStruct((rows, d), F32),
        compiler_params=_params(("arbitrary", "arbitrary"), 60),
        name="moe_down",
    )(block_expert, n_used, h1, w_down, b_down.reshape(ne, 1, d))


def _combine_kernel(slot_ref, x_ref, gate_ref, yb_ref, oa_ref, ob_ref, buf_sc, sem, *, tc, n, na):
    i = pl.program_id(0)
    nsteps = pl.num_programs(0)

    def row_copy(buf, k, r, s):
        return pltpu.make_async_copy(yb_ref.at[pl.ds(s, 1)], buf_sc.at[buf, k, pl.ds(r, 1)], sem.at[buf])

    def issue(step, buf):
        def body(r, carry):
            for k in range(TOP_K):
                row_copy(buf, k, r, slot_ref[k * n + step * tc + r]).start()
            return carry

        lax.fori_loop(0, tc, body, 0)

    def drain(buf):
        def body(r, carry):
            for k in range(TOP_K):
                row_copy(buf, k, 0, 0).wait()
            return carry

        lax.fori_loop(0, tc, body, 0)

    cur = i & 1

    @pl.when(i == 0)
    def _():
        issue(0, 0)

    @pl.when(i + 1 < nsteps)
    def _():
        issue(i + 1, 1 - cur)

    drain(cur)
    gates = gate_ref[...]
    acc = x_ref[...]
    for k in range(TOP_K):
        acc = acc + gates[:, k:k + 1] * buf_sc[cur, k]

    @pl.when(i < na)
    def _():
        oa_ref[...] = acc

    @pl.when(i >= na)
    def _():
        ob_ref[...] = acc


def _combine(slot_flat, x1, gates, yb, *, n_first, tc):
    n, d = x1.shape
    assert n_first % tc == 0 and (n - n_first) % tc == 0
    na = n_first // tc
    amap, bmap = _two_group_maps(na)
    kern = functools.partial(_combine_kernel, tc=tc, n=n, na=na)
    return pl.pallas_call(
        kern,
        grid_spec=pltpu.PrefetchScalarGridSpec(
            num_scalar_prefetch=1,
            grid=(n // tc,),
            in_specs=[
                pl.BlockSpec((tc, d), lambda i, *_: (i, 0)),
                pl.BlockSpec((tc, TOP_K), lambda i, *_: (i, 0)),
                pl.BlockSpec(memory_space=pl.ANY),
            ],
            out_specs=[
                pl.BlockSpec((tc, d), lambda i, *_: (amap(i), 0)),
                pl.BlockSpec((tc, d), lambda i, *_: (bmap(i), 0)),
            ],
            scratch_shapes=[pltpu.VMEM((2, TOP_K, tc, d), F32), pltpu.SemaphoreType.DMA((2,))],
        ),
        out_shape=[
            jax.ShapeDtypeStruct((n_first, d), F32),
            jax.ShapeDtypeStruct((n - n_first, d), F32),
        ],
        compiler_params=_params(("arbitrary",), 48),
        name="combine",
    )(slot_flat, x1, gates, yb)


def _rotate_half_cols(w):
    half = w.shape[-1] // 2
    return jnp.concatenate([-w[..., half:], w[..., :half]], axis=-1)


def _rope_tables(pos):
    inv = 1.0 / (ROPE_THETA ** (jnp.arange(0, QK_ROPE, 2, dtype=F32) / QK_ROPE))
    ang = pos.astype(F32)[:, None] * inv[None, :]
    cos, sin = jnp.cos(ang), jnp.sin(ang)
    return jnp.concatenate([cos, cos], axis=-1), jnp.concatenate([sin, sin], axis=-1)


def kernel(x_prompt, x_sample, cache_ckv, cache_kpe, state_pool, attn_norm, w_in, q_a_norm, w_q_b, kv_a_norm, w_kv_b, q_norm, k_norm, w_pool, pool_scale, w_out, ffn_norm, w_router, b_router, w_gate, b_gate, w_up, b_up, w_down, b_down):
    bp, seq, d = x_prompt.shape
    nb, t_dec, _ = x_sample.shape
    past = cache_ckv.shape[1]
    kv_lora = cache_ckv.shape[2]
    d_pool = state_pool.shape[2]
    q_lora = q_a_norm.shape[0]
    nh = w_q_b.shape[1] // QK_HEAD
    ne = w_router.shape[1]
    assert bp == 1 and seq >= POOL_STATE and t_dec >= POOL_STATE
    assert w_in.shape[1] == d_pool + q_lora + kv_lora + QK_ROPE

    n_p, n_s = bp * seq, nb * t_dec
    n = n_p + n_s
    tm = _tile(math.gcd(n_p, n_s), 512)
    xa, xb = x_prompt.reshape(n_p, d), x_sample.reshape(n_s, d)

    cos_p, sin_p = _rope_tables(jnp.arange(seq))
    cos_s, sin_s = _rope_tables(past + jnp.arange(t_dec))
    cos2 = jnp.concatenate([cos_p] * bp + [cos_s] * nb, axis=0)
    sin2 = jnp.concatenate([sin_p] * bp + [sin_s] * nb, axis=0)

    tn_in = _tile(kv_lora, 512, 128)
    while d_pool % tn_in or q_lora % tn_in:
        tn_in //= 2
    assert tn_in >= 2 * QK_ROPE
    w_kpe = w_in[:, d_pool + q_lora + kv_lora:]
    w_all = jnp.concatenate(
        [w_in[:, :d_pool + q_lora + kv_lora], w_kpe, _rotate_half_cols(w_kpe),
         jnp.zeros((d, tn_in - 2 * QK_ROPE), w_in.dtype)], axis=1).astype(BF16)
    wq = w_q_b.reshape(q_lora, nh, QK_HEAD)
    wq = jnp.concatenate([wq, _rotate_half_cols(wq[..., QK_NOPE:])], axis=-1)
    wq = wq.transpose(1, 0, 2).astype(BF16)
    wkv = w_kv_b.reshape(kv_lora, nh, QK_NOPE + V_HEAD).transpose(1, 0, 2).astype(BF16)
    w_out_b = w_out.astype(BF16)

    u, cqn, ckv, kpe = _inproj(xa, xb, attn_norm.reshape(1, d), w_all, q_a_norm.reshape(1, q_lora),
                               kv_a_norm.reshape(1, kv_lora), cos2, sin2,
                               d_pool=d_pool, q_lora=q_lora, kv_lora=kv_lora, tn=tn_in, tm=tm)
    hb = 4 if nh % 4 == 0 else (2 if nh % 2 == 0 else 1)
    q = _qproj(cqn, wq, q_norm.reshape(1, QK_HEAD), cos2, sin2, tm=tm, hb=hb)
    t_att = _tile(math.gcd(seq, n), 512, CHUNK)
    k_new, v_new, vt_new = _kvexp(ckv, kpe, wkv, k_norm.reshape(1, QK_HEAD), tm=t_att, hb=hb, with_vt=True)
    n_c = nb * past
    k_past, v_past = _kvexp(cache_ckv.reshape(n_c, kv_lora), cache_kpe.reshape(n_c, QK_ROPE), wkv,
                            k_norm.reshape(1, QK_HEAD), tm=_tile(n_c, 512), hb=hb, with_vt=False)

    attn_p = _attn_prompt(q, k_new, vt_new, seq=seq, tk=t_att)
    attn_s = _attn_sample(q, k_new, v_new, k_past, v_past, nb=nb, t=t_dec, past=past, row0=n_p, hb=hb)

    u_p = u[:n_p].reshape(bp, seq, d_pool)
    u_s = u[n_p:].reshape(nb, t_dec, d_pool)
    w_pool_b = w_pool.astype(BF16)
    scale2 = pool_scale.reshape(1, d_pool)
    halo_p = jnp.zeros((bp, POOL_HALO, d_pool), F32)
    halo_s = jnp.concatenate([jnp.zeros((nb, 1, d_pool), F32), state_pool], axis=1)
    pool_p = _pool(u_p, halo_p, w_pool_b, scale2, pos0=0, tm=_tile(seq, 512))
    pool_s = _pool(u_s, halo_s, w_pool_b, scale2, pos0=past, tm=t_dec)

    x1 = _outproj(pool_p, pool_s, attn_p, attn_s, w_out_b[:d_pool], w_out_b[d_pool:], xa, xb,
                  tm=tm, tn=_tile(d, 1024, 128))

    hp, top_idx, gates, rank, counts = _router(x1, ffn_norm.reshape(1, d), w_router.T.astype(BF16),
                                               b_router.reshape(ne, 1), tm=tm)
    tm_moe = 512
    counts = counts.reshape(ne)
    padded = (counts + tm_moe - 1) // tm_moe * tm_moe
    pend = jnp.cumsum(padded)
    pstart = pend - padded
    experts = jnp.arange(ne, dtype=jnp.int32)
    first_slot = jnp.sum(jnp.where(top_idx[..., None] == experts, pstart.astype(jnp.int32), 0), axis=-1)
    slot_flat = (first_slot + rank).reshape(TOP_K * n).astype(jnp.int32)
    n_blocks = -(-(n * TOP_K) // tm_moe) + ne
    block_row0 = jnp.arange(n_blocks, dtype=jnp.int32) * tm_moe
    block_expert = jnp.minimum(jnp.sum(block_row0[:, None] >= pend[None, :], axis=1), ne - 1).astype(jnp.int32)
    n_used = (pend[-1:] // tm_moe).astype(jnp.int32)

    xb = _dispatch(slot_flat, (pstart + counts).astype(jnp.int32), (padded - counts).astype(jnp.int32), n_used, hp,
                   tm=tm_moe, nblk=n_blocks, tt=_tile(n, 256, 8))
    dff = w_gate.shape[2]
    h1 = _gateup(block_expert, n_used, xb, w_gate, w_up, b_gate, b_up, tm=tm_moe, tn=_tile(dff, 512, 128))
    yb = _down(block_expert, n_used, h1, w_down, b_down, tm=tm_moe, tn=_tile(d, 1024, 128))
    y_p, y_s = _combine(slot_flat, x1, gates.T, yb, n_first=n_p, tc=_tile(math.gcd(n_p, n_s), 128, 8))

    return (
        y_p.reshape(bp, seq, d),
        y_s.reshape(nb, t_dec, d),
        ckv[:n_p].reshape(bp, seq, kv_lora),
        kpe[:n_p].reshape(bp, seq, QK_ROPE),
        u_p[:, seq - POOL_STATE:],
        ckv[n_p:].reshape(nb, t_dec, kv_lora),
        kpe[n_p:].reshape(nb, t_dec, QK_ROPE),
        u_s[:, t_dec - POOL_STATE:],
    )
```
